```python
import math
import jax, jax.numpy as jnp
from jax import lax
import numpy as np

D_MODEL = 1024
BATCH = 4
SEQ = 4096
DEPTH = 2

GRID_W = 64
CTX_LEN = 256
HEAD_DIM = 64
AXIS_PAIRS = HEAD_DIM // 4
ROPE_THETA = 10000.0
Q_BLOCK = 128
EPS = 1e-6
NEG_INF = -1e30

A_HEADS = 4
A_V_DIM = 2 * HEAD_DIM
B_HEADS = 8
B_KV_HEADS = 2
C_HEADS = 8
C_KV_HEADS = 2
WINDOW = 128

A_QK_W = A_HEADS * 2 * HEAD_DIM
A_V_W = A_HEADS * A_V_DIM
B_Q_W = B_HEADS * HEAD_DIM
B_KV_W = B_KV_HEADS * HEAD_DIM
C_Q_W = C_HEADS * HEAD_DIM
C_KV_W = C_KV_HEADS * HEAD_DIM
N_BRANCH = 3
BRANCH_W = 512
GATE_W = N_BRANCH * D_MODEL
IN_SIZES = (A_QK_W, A_QK_W, A_V_W, B_Q_W, B_KV_W, B_KV_W, C_Q_W, C_KV_W, C_KV_W, GATE_W)
IN_COLS = A_QK_W + A_QK_W + A_V_W + B_Q_W + B_KV_W + B_KV_W + C_Q_W + C_KV_W + C_KV_W + GATE_W

N_EXPERTS = 32
TOP_K = 4
D_FF_EXPERT = D_MODEL
SWIGLU_LIMIT = 7.0
SWIGLU_ALPHA = 1.702
MOE_BLOCK = 128

kernel_name = "hybrid_diffattn_gqa_swa_moe_dit_block"


def rms_norm(x, g):
    xf = x.astype(jnp.float32)
    y = xf * lax.rsqrt(jnp.mean(xf * xf, axis=-1, keepdims=True) + EPS)
    return (y * g.astype(jnp.float32)).astype(x.dtype)


def axial_rope_tables(n_tokens, dtype):
    rows = n_tokens // GRID_W
    row = jnp.repeat(jnp.arange(rows, dtype=jnp.float32), GRID_W)
    col = jnp.tile(jnp.arange(GRID_W, dtype=jnp.float32), rows)
    inv_freq = ROPE_THETA ** (-jnp.arange(AXIS_PAIRS, dtype=jnp.float32) / AXIS_PAIRS)
    ang_r = row[:, None] * inv_freq
    ang_c = col[:, None] * inv_freq
    ang = jnp.concatenate([ang_r, ang_r, ang_c, ang_c], axis=-1)
    return jnp.cos(ang).astype(dtype), jnp.sin(ang).astype(dtype)


def apply_rope(x, cos, sin):
    bshape = (cos.shape[0],) + (1,) * (x.ndim - 3) + (HEAD_DIM,)
    xr = x.reshape(x.shape[:-1] + (2, 2, AXIS_PAIRS))
    rot = jnp.stack([-xr[..., 1, :], xr[..., 0, :]], axis=-2).reshape(x.shape)
    return x * cos.reshape(bshape) + rot * sin.reshape(bshape)


def over_query_blocks(fn, q):
    b, s = q.shape[:2]
    nb = s // Q_BLOCK
    qb = jnp.moveaxis(q.reshape((b, nb, Q_BLOCK) + q.shape[2:]), 1, 0)
    ob = lax.map(lambda a: fn(a[0], a[1]), (qb, jnp.arange(nb)))
    ob = jnp.moveaxis(ob, 0, 1)
    return ob.reshape((b, s) + ob.shape[3:])


def diff_attend(q, k, v, lam):
    s = jnp.einsum('bqhmd,bkhmd->bhmqk', q, k).astype(jnp.float32) * (HEAD_DIM ** -0.5)
    p = jax.nn.softmax(s, axis=-1)
    a = (p[:, :, 0] - lam * p[:, :, 1]).astype(v.dtype)
    return jnp.einsum('bhqk,bkhe->bqhe', a, v)


def gqa_attend(q, k, v):
    s = jnp.einsum('bqngd,bknd->bngqk', q, k).astype(jnp.float32) * (HEAD_DIM ** -0.5)
    p = jax.nn.softmax(s, axis=-1).astype(v.dtype)
    return jnp.einsum('bngqk,bknd->bqngd', p, v)


def sink_attend(q, segments, sink):
    scores = []
    for k, v, mask in segments:
        s = jnp.einsum('bqngd,bknd->bngqk', q, k).astype(jnp.float32) * (HEAD_DIM ** -0.5)
        if mask is not None:
            s = jnp.where(mask, s, NEG_INF)
        scores.append(s)
    b, nq = q.shape[:2]
    n, g = sink.shape
    sink_col = jnp.broadcast_to(sink.astype(jnp.float32)[None, :, :, None, None], (b, n, g, nq, 1))
    p = jax.nn.softmax(jnp.concatenate(scores + [sink_col], axis=-1), axis=-1)
    out = None
    offset = 0
    for k, v, _ in segments:
        kl = k.shape[1]
        o = jnp.einsum('bngqk,bknd->bqngd', p[..., offset:offset + kl].astype(v.dtype), v)
        out = o if out is None else out + o
        offset += kl
    return out


def split_projection(p):
    b, t = p.shape[:2]
    aq, ak, av, bq, bk, bv, cq, ck, cv, gl = jnp.split(p, np.cumsum(IN_SIZES)[:-1].tolist(), axis=-1)
    return (aq.reshape(b, t, A_HEADS, 2, HEAD_DIM),
            ak.reshape(b, t, A_HEADS, 2, HEAD_DIM),
            av.reshape(b, t, A_HEADS, A_V_DIM),
            bq.reshape(b, t, B_KV_HEADS, B_HEADS // B_KV_HEADS, HEAD_DIM),
            bk.reshape(b, t, B_KV_HEADS, HEAD_DIM),
            bv.reshape(b, t, B_KV_HEADS, HEAD_DIM),
            cq.reshape(b, t, C_KV_HEADS, C_HEADS // C_KV_HEADS, HEAD_DIM),
            ck.reshape(b, t, C_KV_HEADS, HEAD_DIM),
            cv.reshape(b, t, C_KV_HEADS, HEAD_DIM),
            gl)


def merge_branches(ya, yb, yc, gate_logits, w_branch, w_out):
    b, t = ya.shape[:2]
    ys = jnp.stack([ya.reshape(b, t, BRANCH_W), yb.reshape(b, t, BRANCH_W), yc.reshape(b, t, BRANCH_W)], axis=2)
    proj = jnp.einsum('btie,ied->btid', ys, w_branch)
    gates = jax.nn.sigmoid(gate_logits.reshape(b, t, N_BRANCH, D_MODEL))
    return jnp.sum(gates * proj, axis=2) @ w_out


def token_mixers(h_lat, h_ctx, w_in, lam_vec, lam_init, subln_g, qk_g, sink, w_branch, w_out, cos, sin, need_ctx):
    n_lat = h_lat.shape[1]
    aq, ak, av, bq, bk, bv, cq, ck, cv, gl = split_projection(h_lat @ w_in)
    aq_c, ak_c, av_c, bq_c, bk_c, bv_c, cq_c, ck_c, cv_c, gl_c = split_projection(h_ctx @ w_in)
    bq, bq_c = rms_norm(bq, qk_g[0]), rms_norm(bq_c, qk_g[0])
    bk, bk_c = rms_norm(bk, qk_g[1]), rms_norm(bk_c, qk_g[1])
    aq, ak = apply_rope(aq, cos, sin), apply_rope(ak, cos, sin)
    bq, bk = apply_rope(bq, cos, sin), apply_rope(bk, cos, sin)
    cq, ck = apply_rope(cq, cos, sin), apply_rope(ck, cos, sin)
    lv = lam_vec.astype(jnp.float32)
    lam = jnp.exp(jnp.sum(lv[0] * lv[1])) - jnp.exp(jnp.sum(lv[2] * lv[3])) + lam_init
    sink_ng = sink.reshape(C_KV_HEADS, C_HEADS // C_KV_HEADS)

    ak_all = jnp.concatenate([ak, ak_c], axis=1)
    av_all = jnp.concatenate([av, av_c], axis=1)
    ya = over_query_blocks(lambda qb, i: diff_attend(qb, ak_all, av_all, lam), aq)
    ya = rms_norm(ya, subln_g) * (1.0 - lam_init)

    bk_all = jnp.concatenate([bk, bk_c], axis=1)
    bv_all = jnp.concatenate([bv, bv_c], axis=1)
    yb = over_query_blocks(lambda qb, i: gqa_attend(qb, bk_all, bv_all), bq)

    span = Q_BLOCK + 2 * WINDOW
    ck_pad = jnp.pad(ck, ((0, 0), (WINDOW, WINDOW), (0, 0), (0, 0)))
    cv_pad = jnp.pad(cv, ((0, 0), (WINDOW, WINDOW), (0, 0), (0, 0)))

    def window_block(qblk, i):
        start = i * Q_BLOCK
        kw = lax.dynamic_slice_in_dim(ck_pad, start, span, axis=1)
        vw = lax.dynamic_slice_in_dim(cv_pad, start, span, axis=1)
        qpos = start + jnp.arange(Q_BLOCK)
        kpos = start - WINDOW + jnp.arange(span)
        mask = (jnp.abs(qpos[:, None] - kpos[None, :]) <= WINDOW) & (kpos >= 0)[None, :] & (kpos < n_lat)[None, :]
        return sink_attend(qblk, [(kw, vw, mask), (ck_c, cv_c, None)], sink_ng)

    yc = over_query_blocks(window_block, cq)
    y_lat = merge_branches(ya, yb, yc, gl, w_branch, w_out)

    if not need_ctx:
        return y_lat, None
    ya_c = rms_norm(diff_attend(aq_c, ak_c, av_c, lam), subln_g) * (1.0 - lam_init)
    yb_c = gqa_attend(bq_c, bk_c, bv_c)
    yc_c = sink_attend(cq_c, [(ck_c, cv_c, None)], sink_ng)
    y_ctx = merge_branches(ya_c, yb_c, yc_c, gl_c, w_branch, w_out)
    return y_lat, y_ctx


def moe_ffn(h, router_w, router_b, w1, b1, w2, b2):
    n, d = h.shape
    logits = (h @ router_w + router_b).astype(jnp.float32)
    top_val, top_idx = lax.top_k(logits, TOP_K)
    comb = jax.nn.softmax(top_val, axis=-1).astype(h.dtype)
    n_slots = n * TOP_K
    flat_e = top_idx.reshape(-1).astype(jnp.int32)
    order = jnp.argsort(flat_e)
    sorted_e = flat_e[order]
    counts = jnp.bincount(flat_e, length=N_EXPERTS).astype(jnp.int32)
    padded = (counts + MOE_BLOCK - 1) // MOE_BLOCK * MOE_BLOCK
    start = jnp.cumsum(counts) - counts
    pad_end = jnp.cumsum(padded)
    pad_start = pad_end - padded
    dest = (pad_start[sorted_e] + jnp.arange(n_slots, dtype=jnp.int32) - start[sorted_e]).astype(jnp.int32)
    n_blocks = -(-n_slots // MOE_BLOCK) + N_EXPERTS
    rows = n_blocks * MOE_BLOCK
    row_tok = jnp.zeros((rows,), jnp.int32).at[dest].set((order // TOP_K).astype(jnp.int32))
    block_e = jnp.minimum(jnp.searchsorted(pad_end, jnp.arange(n_blocks, dtype=jnp.int32) * MOE_BLOCK, side='right'), N_EXPERTS - 1)
    xb = h[row_tok].reshape(n_blocks, MOE_BLOCK, d)

    def expert_block(args):
        xe, e = args
        gu = xe @ w1[e] + b1[e]
        gate, up = jnp.split(gu, 2, axis=-1)
        gate = jnp.minimum(gate, SWIGLU_LIMIT)
        up = jnp.clip(up, -SWIGLU_LIMIT, SWIGLU_LIMIT)
        act = (up + 1.0) * (gate * jax.nn.sigmoid(SWIGLU_ALPHA * gate))
        return act @ w2[e] + b2[e]

    yb = lax.map(expert_block, (xb, block_e)).reshape(rows, d)
    slot_row = jnp.zeros((n_slots,), jnp.int32).at[order].set(dest)
    y = yb[slot_row].reshape(n, TOP_K, d)
    return jnp.einsum('nk,nkd->nd', comb, y)


def setup_inputs(seed: int = 0) -> dict:
    key = jax.random.key(seed)
    ks = jax.random.split(key, 20)
    f32 = jnp.float32
    D = D_MODEL

    def nrm(k, shape, scale):
        return jax.random.normal(k, shape, f32) * scale

    return {
        "x": nrm(ks[0], (BATCH, SEQ, D), 1.0),
        "c": nrm(ks[1], (BATCH, D), 1.0),
        "ctx": nrm(ks[2], (BATCH, CTX_LEN, D), 1.0),
        "c_ctx": nrm(ks[3], (D,), 1.0),
        "ada_w": nrm(ks[4], (DEPTH, D, 6 * D), 0.5 * D ** -0.5),
        "ada_b": nrm(ks[5], (DEPTH, 6 * D), 0.01),
        "norm_g": 1.0 + nrm(ks[6], (DEPTH, 4, D), 0.05),
        "w_in": nrm(ks[7], (DEPTH, D, IN_COLS), D ** -0.5),
        "diff_lambda": nrm(ks[8], (DEPTH, 4, HEAD_DIM), 0.1),
        "diff_subln": 1.0 + nrm(ks[9], (DEPTH, A_V_DIM), 0.05),
        "qk_norm": 1.0 + nrm(ks[10], (DEPTH, 2, HEAD_DIM), 0.05),
        "sink": nrm(ks[11], (DEPTH, C_HEADS), 1.0),
        "w_branch": nrm(ks[12], (DEPTH, N_BRANCH, BRANCH_W, D), BRANCH_W ** -0.5),
        "w_out": nrm(ks[13], (DEPTH, D, D), D ** -0.5),
        "router_w": nrm(ks[14], (DEPTH, D, N_EXPERTS), D ** -0.5),
        "router_b": nrm(ks[15], (DEPTH, N_EXPERTS), 0.01),
        "w1": nrm(ks[16], (DEPTH, N_EXPERTS, D, 2 * D_FF_EXPERT), D ** -0.5),
        "b1": nrm(ks[17], (DEPTH, N_EXPERTS, 2 * D_FF_EXPERT), 0.01),
        "w2": nrm(ks[18], (DEPTH, N_EXPERTS, D_FF_EXPERT, D), D_FF_EXPERT ** -0.5),
        "b2": nrm(ks[19], (DEPTH, N_EXPERTS, D), 0.01),
    }


def reference(x, c, ctx, c_ctx, ada_w, ada_b, norm_g, w_in, diff_lambda, diff_subln, qk_norm, sink,
              w_branch, w_out, router_w, router_b, w1, b1, w2, b2):
    b, s, d = x.shape
    cos, sin = axial_rope_tables(s, x.dtype)
    for l in range(DEPTH):
        last = l == DEPTH - 1
        lam_init = 0.8 - 0.6 * math.exp(-0.3 * l)
        mod_lat = (jax.nn.silu(c) @ ada_w[l] + ada_b[l])[:, None, :]
        mod_ctx = (jax.nn.silu(c_ctx) @ ada_w[l] + ada_b[l])[None, None, :]
        sh_m, sc_m, g_m, sh_f, sc_f, g_f = jnp.split(mod_lat, 6, axis=-1)
        csh_m, csc_m, cg_m, csh_f, csc_f, cg_f = jnp.split(mod_ctx, 6, axis=-1)

        h_lat = rms_norm(x, norm_g[l, 0]) * (1.0 + sc_m) + sh_m
        h_ctx = rms_norm(ctx, norm_g[l, 0]) * (1.0 + csc_m) + csh_m
        y_lat, y_ctx = token_mixers(h_lat, h_ctx, w_in[l], diff_lambda[l], lam_init, diff_subln[l], qk_norm[l],
                                    sink[l], w_branch[l], w_out[l], cos, sin, not last)
        x = x + g_m * rms_norm(y_lat, norm_g[l, 1])

        h_lat = rms_norm(x, norm_g[l, 2]) * (1.0 + sc_f) + sh_f
        if last:
            y = moe_ffn(h_lat.reshape(b * s, d), router_w[l], router_b[l], w1[l], b1[l], w2[l], b2[l])
            x = x + g_f * rms_norm(y.reshape(b, s, d), norm_g[l, 3])
        else:
            ctx = ctx + cg_m * rms_norm(y_ctx, norm_g[l, 1])
            h_ctx = rms_norm(ctx, norm_g[l, 2]) * (1.0 + csc_f) + csh_f
            tokens = jnp.concatenate([h_lat.reshape(b * s, d), h_ctx.reshape(-1, d)], axis=0)
            y = moe_ffn(tokens, router_w[l], router_b[l], w1[l], b1[l], w2[l], b2[l])
            x = x + g_f * rms_norm(y[:b * s].reshape(b, s, d), norm_g[l, 3])
            ctx = ctx + cg_f * rms_norm(y[b * s:].reshape(ctx.shape), norm_g[l, 3])
    return x
```

```python
import functools
import math

import jax
import jax.numpy as jnp
import numpy as np
from jax import lax
from jax.experimental import pallas as pl
from jax.experimental.pallas import tpu as pltpu

F32 = jnp.float32
BF16 = jnp.bfloat16

HEAD_DIM = 64
LANES = 128
GRID_W = 64
AXIS_PAIRS = HEAD_DIM // 4
ROPE_THETA = 10000.0
EPS = 1e-6
NEG_INF = -1e30
WINDOW = 128
A_HEADS = 4
KV_HEADS = 2
Q_GROUP = 4
N_EXPERTS = 32
TOP_K = 4
SWIGLU_LIMIT = 7.0
SWIGLU_ALPHA = 1.702
Q_SCALE = HEAD_DIM ** -0.5
PROJ_TN = 512
MOE_TM = 256
VMEM_LIMIT = 56 * 1024 * 1024

T_AQ, T_AK, T_BQ, T_CQ, T_KV, T_AV, T_GATE = 0, 1, 2, 3, 4, 5, 6


def _cparams(sem):
    return pltpu.CompilerParams(dimension_semantics=sem, vmem_limit_bytes=VMEM_LIMIT)


def _rms(x, g):
    return x * lax.rsqrt(jnp.mean(x * x, axis=-1, keepdims=True) + EPS) * g


def _sigmoid(x):
    return 1.0 / (1.0 + jnp.exp(-x))


def _pow2_tile(cap, *sizes):
    t = cap
    while any(s % t for s in sizes):
        t //= 2
    return t


def _adaln_kernel(c_ref, w_ref, b_ref, o_ref):
    c = c_ref[...]
    a = c * _sigmoid(c)
    o_ref[0] = jnp.dot(a, w_ref[0], preferred_element_type=F32,
                       precision=lax.Precision.HIGHEST) + b_ref[0]


def _adaln(cc, ada_w, ada_b):
    depth, d, n = ada_w.shape
    rows = cc.shape[0]
    tn = _pow2_tile(1024, n // 6)
    return pl.pallas_call(
        _adaln_kernel,
        grid=(depth, n // tn),
        in_specs=[pl.BlockSpec((rows, d), lambda l, j: (0, 0)),
                  pl.BlockSpec((1, d, tn), lambda l, j: (l, 0, j)),
                  pl.BlockSpec((1, 1, tn), lambda l, j: (l, 0, j))],
        out_specs=pl.BlockSpec((1, rows, tn), lambda l, j: (l, 0, j)),
        out_shape=jax.ShapeDtypeStruct((depth, rows, n), F32),
        compiler_params=_cparams(("parallel", "parallel")),
        name="adaln",
    )(cc, ada_w, ada_b.reshape(depth, 1, n))


def _norm_mod_kernel(x_ref, g_ref, sc_ref, sh_ref, h_ref):
    h = _rms(x_ref[...], g_ref[...]) * (1.0 + sc_ref[0]) + sh_ref[0]
    h_ref[...] = h.astype(h_ref.dtype)


def _seg_map(tiles_per_batch, n_batch):
    return lambda i: (jnp.minimum(i // tiles_per_batch, n_batch), 0, 0)


def _norm_mod(xc, g, sc, sh, seq, n_batch):
    n, d = xc.shape
    tm = _pow2_tile(1024, seq, n - n_batch * seq)
    seg = _seg_map(seq // tm, n_batch)
    return pl.pallas_call(
        _norm_mod_kernel,
        grid=(n // tm,),
        in_specs=[pl.BlockSpec((tm, d), lambda i: (i, 0)),
                  pl.BlockSpec((1, d), lambda i: (0, 0)),
                  pl.BlockSpec((1, 1, d), seg),
                  pl.BlockSpec((1, 1, d), seg)],
        out_specs=pl.BlockSpec((tm, d), lambda i: (i, 0)),
        out_shape=jax.ShapeDtypeStruct((n, d), BF16),
        compiler_params=_cparams(("parallel",)),
        name="norm_mod",
    )(xc, g.reshape(1, d), sc, sh)


_ROPE_S, _ROPE, _NORMQ, _NORMK, _PLAIN = range(5)
_TILE_OPS = {T_AQ: [_ROPE_S] * 4, T_AK: [_ROPE] * 4, T_BQ: [_NORMQ] * 4, T_CQ: [_ROPE_S] * 4,
             T_KV: [_NORMK, _ROPE, _PLAIN, _PLAIN]}


def _proj_kernel(h_ref, w_ref, cos_ref, sa_ref, sb_ref, gq_ref, gk_ref, o_ref, acc_ref):
    j = pl.program_id(1)
    acc_ref[...] = jnp.dot(h_ref[...], w_ref[...], preferred_element_type=F32)

    def rope(x):
        return (x * cos_ref[...] + pltpu.roll(x, LANES - 16, 1) * sa_ref[...]
                + pltpu.roll(x, 16, 1) * sb_ref[...])

    def head_norm(x, g):
        lo = lax.broadcasted_iota(jnp.int32, x.shape, 1) < HEAD_DIM
        x2 = x * x
        s_lo = jnp.sum(jnp.where(lo, x2, 0.0), axis=1, keepdims=True)
        s_hi = jnp.sum(jnp.where(lo, 0.0, x2), axis=1, keepdims=True)
        ms = jnp.where(lo, s_lo, s_hi) * (1.0 / HEAD_DIM)
        return x * lax.rsqrt(ms + EPS) * g

    def chunk(op, x):
        if op == _ROPE_S:
            return rope(x) * Q_SCALE
        if op == _ROPE:
            return rope(x)
        if op == _NORMQ:
            return rope(head_norm(x, gq_ref[...])) * Q_SCALE
        if op == _NORMK:
            return rope(head_norm(x, gk_ref[...]))
        return x

    for t, ops in _TILE_OPS.items():
        @pl.when(j == t)
        def _(ops=ops):
            for c, op in enumerate(ops):
                cs = slice(c * LANES, (c + 1) * LANES)
                o_ref[:, cs] = chunk(op, acc_ref[:, cs]).astype(o_ref.dtype)

    @pl.when(j >= T_AV)
    def _():
        o_ref[...] = acc_ref[...].astype(o_ref.dtype)


def _proj(h, w, cos, sa, sb, gq, gk):
    n, d = h.shape
    cols = w.shape[1]
    tm = _pow2_tile(1024, n)
    row = lambda i, j: (i, 0)
    vec = lambda i, j: (0, 0)
    return pl.pallas_call(
        _proj_kernel,
        grid=(n // tm, cols // PROJ_TN),
        in_specs=[pl.BlockSpec((tm, d), row),
                  pl.BlockSpec((d, PROJ_TN), lambda i, j: (0, j)),
                  pl.BlockSpec((tm, LANES), row),
                  pl.BlockSpec((tm, LANES), row),
                  pl.BlockSpec((tm, LANES), row),
                  pl.BlockSpec((1, LANES), vec),
                  pl.BlockSpec((1, LANES), vec)],
        out_specs=pl.BlockSpec((tm, PROJ_TN), lambda i, j: (i, j)),
        out_shape=jax.ShapeDtypeStruct((n, cols), BF16),
        scratch_shapes=[pltpu.VMEM((tm, PROJ_TN), F32)],
        compiler_params=_cparams(("parallel", "arbitrary")),
        name="proj",
    )(h, w, cos, sa, sb, gq, gk)


def _lane_lo(shape):
    return lax.broadcasted_iota(jnp.int32, shape, 1) < HEAD_DIM


def _split_heads(qv):
    lo = _lane_lo(qv.shape)
    zero = jnp.zeros_like(qv)
    return jnp.concatenate([jnp.where(lo, qv, zero), jnp.where(lo, zero, qv)], axis=0)


def _qk(lhs, k):
    return lax.dot_general(lhs, k, (((1,), (1,)), ((), ())), preferred_element_type=F32)


def _online_step(carry, lhs, k, v):
    m, l, acc = carry
    s = _qk(lhs, k)
    m_new = jnp.maximum(m, jnp.max(s, axis=1, keepdims=True))
    alpha = jnp.exp(m - m_new)
    p = jnp.exp(s - m_new)
    l = alpha * l + jnp.sum(p, axis=1, keepdims=True)
    acc = alpha * acc + jnp.dot(p.astype(v.dtype), v, preferred_element_type=F32)
    return m_new, l, acc


def _flash(lhs, segs, tk):
    rows = lhs.shape[0]
    carry = (jnp.full((rows, 1), NEG_INF, F32), jnp.zeros((rows, 1), F32), jnp.zeros((rows, LANES), F32))
    for k_ref, v_ref in segs:
        nk = k_ref.shape[0]
        t = min(tk, nk)
        if nk == t:
            carry = _online_step(carry, lhs, k_ref[...], v_ref[...])
        else:
            def body(i, c, k_ref=k_ref, v_ref=v_ref, t=t):
                ks = pl.ds(pl.multiple_of(i * t, t), t)
                return _online_step(c, lhs, k_ref[ks, :], v_ref[ks, :])
            carry = lax.fori_loop(0, nk // t, body, carry)
    _, l, acc = carry
    return acc / l


def _attn_a_kernel(lam_init, n_seg, tk, q_ref, dl_ref, sg_ref, *refs):
    segs = [(refs[2 * i], refs[2 * i + 1]) for i in range(n_seg)]
    o_ref = refs[2 * n_seg]
    tq = q_ref.shape[0]
    o = _flash(_split_heads(q_ref[...]), segs, tk)
    dl = dl_ref[...]
    lam = (jnp.exp(jnp.sum(dl[0:1] * dl[1:2], axis=1, keepdims=True))
           - jnp.exp(jnp.sum(dl[2:3] * dl[3:4], axis=1, keepdims=True)) + lam_init)
    y = o[:tq] - lam * o[tq:]
    o_ref[...] = (_rms(y, sg_ref[...]) * (1.0 - lam_init)).astype(o_ref.dtype)


def _gqa_lhs(q_ref):
    return jnp.concatenate([_split_heads(q_ref[:, j * LANES:(j + 1) * LANES]) for j in range(Q_GROUP)], axis=0)


def _gqa_store(o, o_ref):
    tq = o_ref.shape[0]
    lo = _lane_lo((tq, LANES))
    for j in range(Q_GROUP):
        a = o[(2 * j) * tq:(2 * j + 1) * tq]
        b = o[(2 * j + 1) * tq:(2 * j + 2) * tq]
        o_ref[:, j * LANES:(j + 1) * LANES] = jnp.where(lo, a, b).astype(o_ref.dtype)


def _attn_b_kernel(n_seg, tk, q_ref, *refs):
    segs = [(refs[2 * i], refs[2 * i + 1]) for i in range(n_seg)]
    o_ref = refs[2 * n_seg]
    _gqa_store(_flash(_gqa_lhs(q_ref), segs, tk), o_ref)


def _attn_c_kernel(span, has_lat, sink_ref, q_ref, *refs):
    if has_lat:
        kl_ref, vl_ref, kc_ref, vc_ref, o_ref = refs
    else:
        kc_ref, vc_ref, o_ref = refs
    tq = q_ref.shape[0]
    lhs = _gqa_lhs(q_ref)
    rows = lhs.shape[0]
    sink = jnp.concatenate(
        [jnp.full((tq, 1), sink_ref[n * Q_GROUP + j], F32) for j in range(Q_GROUP) for n in range(KV_HEADS)], axis=0)
    kc = kc_ref[...]
    vc = vc_ref[...]
    s_c = _qk(lhs, kc)
    m = jnp.maximum(jnp.max(s_c, axis=1, keepdims=True), sink)
    if has_lat:
        n_lat = kl_ref.shape[0]
        q0 = pl.program_id(1) * tq
        start = pl.multiple_of(jnp.clip(q0 - WINDOW, 0, n_lat - span), WINDOW)
        kw = kl_ref[pl.ds(start, span), :]
        vw = vl_ref[pl.ds(start, span), :]
        s_w = _qk(lhs, kw)
        qi = lax.broadcasted_iota(jnp.int32, (rows, span), 0) & (tq - 1)
        ki = lax.broadcasted_iota(jnp.int32, (rows, span), 1)
        dist = qi - ki + (q0 - start)
        s_w = jnp.where(jnp.abs(dist) <= WINDOW, s_w, NEG_INF)
        m = jnp.maximum(m, jnp.max(s_w, axis=1, keepdims=True))
    p_c = jnp.exp(s_c - m)
    l = jnp.sum(p_c, axis=1, keepdims=True) + jnp.exp(sink - m)
    acc = jnp.dot(p_c.astype(vc.dtype), vc, preferred_element_type=F32)
    if has_lat:
        p_w = jnp.exp(s_w - m)
        l = l + jnp.sum(p_w, axis=1, keepdims=True)
        acc = acc + jnp.dot(p_w.astype(vw.dtype), vw, preferred_element_type=F32)
    _gqa_store(acc / l, o_ref)


def _mixers(p, seq, ctx_len, n_batch, lam_init, diff_lambda, subln, sink_perm, ctx_queries):
    n_lat = n_batch * seq
    cb = PROJ_TN // LANES
    if ctx_queries:
        tq, nq = ctx_len, 1
        q_row = lambda b, i: n_lat // ctx_len + b
    else:
        tq = _pow2_tile(128, seq)
        nq = seq // tq
        q_row = lambda b, i: b * nq + i
    out_rows = n_batch * nq * tq
    tk = 512

    def kv_specs(col, grid_rank):
        def wrap(f):
            return (lambda b, h, i: f(b, h)) if grid_rank == 3 else (lambda b, i: f(b, 0))
        specs = []
        if not ctx_queries:
            specs.append(pl.BlockSpec((seq, LANES), wrap(lambda b, h: (b, col(h)))))
        specs.append(pl.BlockSpec((ctx_len, LANES), wrap(lambda b, h: (n_lat // ctx_len + b, col(h)))))
        return specs

    n_seg = 1 if ctx_queries else 2

    ka = kv_specs(lambda h: T_AK * cb + h, 3)
    va = kv_specs(lambda h: T_AV * cb + h, 3)
    ya = pl.pallas_call(
        functools.partial(_attn_a_kernel, lam_init, n_seg, tk),
        grid=(n_batch, A_HEADS, nq),
        in_specs=[pl.BlockSpec((tq, LANES), lambda b, h, i: (q_row(b, i), T_AQ * cb + h)),
                  pl.BlockSpec(diff_lambda.shape, lambda b, h, i: (0, 0)),
                  pl.BlockSpec((1, LANES), lambda b, h, i: (0, 0))]
        + [s for kv in zip(ka, va) for s in kv],
        out_specs=pl.BlockSpec((tq, LANES), lambda b, h, i: (b * nq + i, h)),
        out_shape=jax.ShapeDtypeStruct((out_rows, A_HEADS * LANES), BF16),
        compiler_params=_cparams(("parallel", "parallel", "arbitrary")),
        name="attn_a_ctx" if ctx_queries else "attn_a",
    )(p, diff_lambda, subln.reshape(1, LANES), *([p] * (2 * n_seg)))

    kb = kv_specs(lambda h: T_KV * cb + 0, 2)
    vb = kv_specs(lambda h: T_KV * cb + 2, 2)
    yb = pl.pallas_call(
        functools.partial(_attn_b_kernel, n_seg, tk),
        grid=(n_batch, nq),
        in_specs=[pl.BlockSpec((tq, PROJ_TN), lambda b, i: (q_row(b, i), T_BQ))]
        + [s for kv in zip(kb, vb) for s in kv],
        out_specs=pl.BlockSpec((tq, PROJ_TN), lambda b, i: (b * nq + i, 0)),
        out_shape=jax.ShapeDtypeStruct((out_rows, PROJ_TN), BF16),
        compiler_params=_cparams(("parallel", "arbitrary")),
        name="attn_b_ctx" if ctx_queries else "attn_b",
    )(p, *([p] * (2 * n_seg)))

    kc = kv_specs(lambda h: T_KV * cb + 1, 2)
    vc = kv_specs(lambda h: T_KV * cb + 3, 2)
    span = min(tq + 2 * WINDOW, seq)
    yc = pl.pallas_call(
        functools.partial(_attn_c_kernel, span, not ctx_queries),
        grid=(n_batch, nq),
        in_specs=[pl.BlockSpec(memory_space=pltpu.SMEM),
                  pl.BlockSpec((tq, PROJ_TN), lambda b, i: (q_row(b, i), T_CQ))]
        + [s for kv in zip(kc, vc) for s in kv],
        out_specs=pl.BlockSpec((tq, PROJ_TN), lambda b, i: (b * nq + i, 0)),
        out_shape=jax.ShapeDtypeStruct((out_rows, PROJ_TN), BF16),
        compiler_params=_cparams(("parallel", "arbitrary")),
        name="attn_c_ctx" if ctx_queries else "attn_c",
    )(sink_perm, p, *([p] * (2 * n_seg)))
    return ya, yb, yc


def _merge_kernel(ya_ref, yb_ref, yc_ref, g0_ref, g1_ref, g2_ref, wb_ref, wo_ref, x_ref, gm_ref, ng1_ref,
                  ng2_ref, scf_ref, shf_ref, rw_ref, rb_ref, xo_ref, ho_ref, lo_ref):
    mix = None
    for y_ref, g_ref, i in ((ya_ref, g0_ref, 0), (yb_ref, g1_ref, 1), (yc_ref, g2_ref, 2)):
        t = _sigmoid(g_ref[...].astype(F32)) * jnp.dot(y_ref[...], wb_ref[i], preferred_element_type=F32)
        mix = t if mix is None else mix + t
    y = jnp.dot(mix.astype(BF16), wo_ref[...], preferred_element_type=F32)
    xn = x_ref[...] + gm_ref[0] * _rms(y, ng1_ref[...])
    xo_ref[...] = xn
    h = _rms(xn, ng2_ref[...]) * (1.0 + scf_ref[0]) + shf_ref[0]
    ho_ref[...] = h.astype(ho_ref.dtype)
    lo_ref[...] = jnp.dot(h, rw_ref[...], preferred_element_type=F32,
                          precision=lax.Precision.HIGHEST) + rb_ref[...]


def _merge(ya, yb, yc, p, wb, wo, xc, gm, ng1, ng2, scf, shf, rw, rb, seq, n_batch):
    n = ya.shape[0]
    d = xc.shape[1]
    tm = _pow2_tile(512, seq, n - n_batch * seq)
    seg = _seg_map(seq // tm, n_batch)
    row = lambda i: (i, 0)
    vec = lambda i: (0, 0)
    gate0 = T_GATE * PROJ_TN // d
    ne = rw.shape[1]
    return pl.pallas_call(
        _merge_kernel,
        grid=(n // tm,),
        in_specs=[pl.BlockSpec((tm, PROJ_TN), row)] * 3
        + [pl.BlockSpec((tm, d), lambda i, k=k: (i, gate0 + k)) for k in range(3)]
        + [pl.BlockSpec(wb.shape, lambda i: (0, 0, 0)),
           pl.BlockSpec(wo.shape, vec),
           pl.BlockSpec((tm, d), row),
           pl.BlockSpec((1, 1, d), seg),
           pl.BlockSpec((1, d), vec),
           pl.BlockSpec((1, d), vec),
           pl.BlockSpec((1, 1, d), seg),
           pl.BlockSpec((1, 1, d), seg),
           pl.BlockSpec(rw.shape, vec),
           pl.BlockSpec((1, ne), vec)],
        out_specs=[pl.BlockSpec((tm, d), row), pl.BlockSpec((tm, d), row), pl.BlockSpec((tm, ne), row)],
        out_shape=[jax.ShapeDtypeStruct((n, d), F32), jax.ShapeDtypeStruct((n, d), BF16),
                   jax.ShapeDtypeStruct((n, ne), F32)],
        compiler_params=_cparams(("parallel",)),
        name="merge",
    )(ya, yb, yc, p, p, p, wb, wo, xc, gm, ng1.reshape(1, d), ng2.reshape(1, d), scf, shf, rw, rb)


def _moe_kernel(be_ref, nb_ref, x_ref, w1_ref, b1_ref, w2_ref, b2_ref, cw_ref, o_ref):
    i = pl.program_id(0)

    @pl.when(i < nb_ref[0])
    def _():
        f = w2_ref.shape[1]
        gu = jnp.dot(x_ref[...], w1_ref[0], preferred_element_type=F32) + b1_ref[0]
        gate = jnp.minimum(gu[:, :f], SWIGLU_LIMIT)
        up = jnp.clip(gu[:, f:], -SWIGLU_LIMIT, SWIGLU_LIMIT)
        act = (up + 1.0) * (gate * _sigmoid(SWIGLU_ALPHA * gate))
        y = jnp.dot(act.astype(BF16), w2_ref[0], preferred_element_type=F32) + b2_ref[0]
        o_ref[...] = y * cw_ref[...]

    @pl.when(i >= nb_ref[0])
    def _():
        o_ref[...] = jnp.zeros_like(o_ref)


def _moe_blocks(xb, row_w, block_e, n_used, w1, b1, w2, b2):
    rows, d = xb.shape
    ne, _, f2 = w1.shape
    f = w2.shape[1]
    nb = rows // MOE_TM
    xrow = lambda i, be, nu: (jnp.minimum(i, nu[0] - 1), 0)
    wexp = lambda i, be, nu: (be[i], 0, 0)
    return pl.pallas_call(
        _moe_kernel,
        grid_spec=pltpu.PrefetchScalarGridSpec(
            num_scalar_prefetch=2,
            grid=(nb,),
            in_specs=[pl.BlockSpec((MOE_TM, d), xrow),
                      pl.BlockSpec((1, d, f2), wexp),
                      pl.BlockSpec((1, 1, f2), wexp),
                      pl.BlockSpec((1, f, d), wexp),
                      pl.BlockSpec((1, 1, d), wexp),
                      pl.BlockSpec((MOE_TM, 1), xrow)],
            out_specs=pl.BlockSpec((MOE_TM, d), lambda i, be, nu: (i, 0)),
        ),
        out_shape=jax.ShapeDtypeStruct((rows, d), F32),
        compiler_params=_cparams(("arbitrary",)),
        name="moe",
    )(block_e, n_used, xb, w1, b1.reshape(ne, 1, f2), w2, b2.reshape(ne, 1, d), row_w)


def _moe(h, logits, w1, b1, w2, b2):
    n, d = h.shape
    top_val, top_idx = lax.top_k(logits, TOP_K)
    comb = jax.nn.softmax(top_val, axis=-1)
    n_slots = n * TOP_K
    flat_e = top_idx.reshape(-1).astype(jnp.int32)
    order = jnp.argsort(flat_e)
    sorted_e = flat_e[order]
    counts = jnp.bincount(flat_e, length=N_EXPERTS).astype(jnp.int32)
    padded = (counts + MOE_TM - 1) // MOE_TM * MOE_TM
    start = jnp.cumsum(counts) - counts
    pad_end = jnp.cumsum(padded)
    pad_start = pad_end - padded
    dest = (pad_start[sorted_e] + jnp.arange(n_slots, dtype=jnp.int32) - start[sorted_e]).astype(jnp.int32)
    nb = -(-n_slots // MOE_TM) + N_EXPERTS
    rows = nb * MOE_TM
    row_tok = jnp.zeros((rows,), jnp.int32).at[dest].set((order // TOP_K).astype(jnp.int32))
    row_w = jnp.zeros((rows,), F32).at[dest].set(comb.reshape(-1)[order])
    block_e = jnp.minimum(jnp.searchsorted(pad_end, jnp.arange(nb, dtype=jnp.int32) * MOE_TM, side='right'),
                          N_EXPERTS - 1).astype(jnp.int32)
    n_used = (pad_end[-1:] // MOE_TM).astype(jnp.int32)
    yb = _moe_blocks(h[row_tok], row_w.reshape(rows, 1), block_e, n_used, w1, b1, w2, b2)
    slot_row = jnp.zeros((n_slots,), jnp.int32).at[order].set(dest)
    return yb[slot_row].reshape(n, TOP_K, d).sum(axis=1)


def _resid_kernel(with_next, x_ref, y_ref, gf_ref, ng3_ref, *refs):
    xn = x_ref[...] + gf_ref[0] * _rms(y_ref[...], ng3_ref[...])
    if with_next:
        ng0_ref, sc_ref, sh_ref, xo_ref, ho_ref = refs
        ho_ref[...] = (_rms(xn, ng0_ref[...]) * (1.0 + sc_ref[0]) + sh_ref[0]).astype(ho_ref.dtype)
    else:
        (xo_ref,) = refs
    xo_ref[...] = xn


def _resid(xc, y, gf, ng3, nxt, seq, n_batch):
    n, d = y.shape
    tm = _pow2_tile(1024, seq, n - n_batch * seq)
    seg = _seg_map(seq // tm, n_batch)
    row = lambda i: (i, 0)
    vec = lambda i: (0, 0)
    in_specs = [pl.BlockSpec((tm, d), row), pl.BlockSpec((tm, d), row), pl.BlockSpec((1, 1, d), seg),
                pl.BlockSpec((1, d), vec)]
    args = [xc, y, gf, ng3.reshape(1, d)]
    out_specs = [pl.BlockSpec((tm, d), row)]
    out_shape = [jax.ShapeDtypeStruct((n, d), F32)]
    if nxt is not None:
        ng0, sc, sh = nxt
        in_specs += [pl.BlockSpec((1, d), vec), pl.BlockSpec((1, 1, d), seg), pl.BlockSpec((1, 1, d), seg)]
        args += [ng0.reshape(1, d), sc, sh]
        out_specs.append(pl.BlockSpec((tm, d), row))
        out_shape.append(jax.ShapeDtypeStruct((n, d), BF16))
    return pl.pallas_call(
        functools.partial(_resid_kernel, nxt is not None),
        grid=(n // tm,),
        in_specs=in_specs, out_specs=out_specs, out_shape=out_shape,
        compiler_params=_cparams(("parallel",)),
        name="resid",
    )(*args)


def _gqa_perm():
    idx = np.arange(KV_HEADS * Q_GROUP * HEAD_DIM).reshape(KV_HEADS, Q_GROUP, HEAD_DIM)
    return idx.transpose(1, 0, 2).reshape(-1)


def _w_in_perm():
    a_qk = A_HEADS * 2 * HEAD_DIM
    sizes = dict(aq=a_qk, ak=a_qk, av=A_HEADS * 2 * HEAD_DIM, bq=512, bk=128, bv=128, cq=512, ck=128, cv=128)
    off, o = {}, 0
    for k in ("aq", "ak", "av", "bq", "bk", "bv", "cq", "ck", "cv"):
        off[k] = o
        o += sizes[k]
    seg = lambda k: off[k] + np.arange(sizes[k])
    g = _gqa_perm()
    return np.concatenate([seg("aq"), seg("ak"), off["bq"] + g, off["cq"] + g,
                           seg("bk"), seg("ck"), seg("bv"), seg("cv"), seg("av")]), o


def _rope_tables(seq, n_ctx_rows):
    rows = seq // GRID_W
    row = jnp.repeat(jnp.arange(rows, dtype=F32), GRID_W)
    col = jnp.tile(jnp.arange(GRID_W, dtype=F32), rows)
    inv_freq = ROPE_THETA ** (-jnp.arange(AXIS_PAIRS, dtype=F32) / AXIS_PAIRS)
    ang_r = row[:, None] * inv_freq
    ang_c = col[:, None] * inv_freq
    ang = jnp.concatenate([ang_r, ang_r, ang_c, ang_c], axis=-1)
    cos, sin = jnp.cos(ang), jnp.sin(ang)
    first_half = (np.arange(HEAD_DIM) % (2 * AXIS_PAIRS)) < AXIS_PAIRS
    sa = jnp.where(first_half, -sin, 0.0)
    sb = jnp.where(first_half, 0.0, sin)
    return [jnp.tile(t, (1, LANES // HEAD_DIM)) for t in (cos, sa, sb)]


def kernel(x, c, ctx, c_ctx, ada_w, ada_b, norm_g, w_in, diff_lambda, diff_subln, qk_norm, sink, w_branch, w_out,
           router_w, router_b, w1, b1, w2, b2):
    n_batch, seq, d = x.shape
    ctx_len = ctx.shape[1]
    depth = ada_w.shape[0]
    n_lat, n_ctx = n_batch * seq, n_batch * ctx_len

    mod_rows = -(-(n_batch + 1) // 8) * 8
    cc = jnp.zeros((mod_rows, d), F32).at[:n_batch].set(c).at[n_batch].set(c_ctx)
    mod = _adaln(cc, ada_w, ada_b).reshape(depth, mod_rows, 6, 1, d).transpose(0, 2, 1, 3, 4)

    perm, n_qkv = _w_in_perm()
    perm = np.concatenate([perm, n_qkv + np.arange(w_in.shape[2] - n_qkv)])
    gperm = _gqa_perm()
    cos_l, sa_l, sb_l = _rope_tables(seq, n_ctx)
    ones = jnp.ones((n_ctx, LANES), F32)
    zeros = jnp.zeros((n_ctx, LANES), F32)
    cos_t = jnp.concatenate([jnp.tile(cos_l, (n_batch, 1)), ones])
    sa_t = jnp.concatenate([jnp.tile(sa_l, (n_batch, 1)), zeros])
    sb_t = jnp.concatenate([jnp.tile(sb_l, (n_batch, 1)), zeros])
    ne_pad = -(-N_EXPERTS // LANES) * LANES

    xc = jnp.concatenate([x.reshape(n_lat, d), ctx.reshape(n_ctx, d)])
    sh_m, sc_m = mod[0, 0], mod[0, 1]
    h = _norm_mod(xc, norm_g[0, 0], sc_m, sh_m, seq, n_batch)

    for l in range(depth):
        last = l == depth - 1
        lam_init = 0.8 - 0.6 * math.exp(-0.3 * l)
        sh_m, sc_m, g_m, sh_f, sc_f, g_f = (mod[l, k] for k in range(6))
        w_l = w_in[l][:, perm].astype(BF16)
        gq = jnp.tile(qk_norm[l, 0], LANES // HEAD_DIM).reshape(1, LANES)
        gk = jnp.tile(qk_norm[l, 1], LANES // HEAD_DIM).reshape(1, LANES)
        p = _proj(h, w_l, cos_t, sa_t, sb_t, gq, gk)

        ya, yb, yc = _mixers(p, seq, ctx_len, n_batch, lam_init, diff_lambda[l], diff_subln[l], sink[l], False)
        if not last:
            ya_c, yb_c, yc_c = _mixers(p, seq, ctx_len, n_batch, lam_init, diff_lambda[l], diff_subln[l], sink[l],
                                       True)
            ya, yb, yc = (jnp.concatenate(t) for t in ((ya, ya_c), (yb, yb_c), (yc, yc_c)))
        n_rows = ya.shape[0]

        wb = jnp.stack([w_branch[l, 0], w_branch[l, 1][gperm], w_branch[l, 2][gperm]]).astype(BF16)
        rw = jnp.zeros((d, ne_pad), F32).at[:, :N_EXPERTS].set(router_w[l])
        rb = jnp.zeros((1, ne_pad), F32).at[0, :N_EXPERTS].set(router_b[l])
        xc, h_f, logits = _merge(ya, yb, yc, p, wb, w_out[l].astype(BF16), xc[:n_rows], g_m, norm_g[l, 1],
                                 norm_g[l, 2], sc_f, sh_f, rw, rb, seq, n_batch)

        y = _moe(h_f, logits[:, :N_EXPERTS], w1[l].astype(BF16), b1[l], w2[l].astype(BF16), b2[l])
        if last:
            (xc,) = _resid(xc, y, g_f, norm_g[l, 3], None, seq, n_batch)
        else:
            xc, h = _resid(xc, y, g_f, norm_g[l, 3], (norm_g[l + 1, 0], mod[l + 1, 1], mod[l + 1, 0]), seq,
                           n_batch)
    return xc[:n_lat].reshape(n_batch, seq, d)
```

```python
import functools
import math

import jax
import jax.numpy as jnp
import numpy as np
from jax import lax
from jax.experimental import pallas as pl
from jax.experimental.pallas import tpu as pltpu

F32 = jnp.float32
BF16 = jnp.bfloat16

HEAD_DIM = 64
LANES = 128
GRID_W = 64
AXIS_PAIRS = HEAD_DIM // 4
ROPE_THETA = 10000.0
EPS = 1e-6
NEG_INF = -1e30
WINDOW = 128
A_HEADS = 4
KV_HEADS = 2
Q_GROUP = 4
N_EXPERTS = 32
TOP_K = 4
SWIGLU_LIMIT = 7.0
SWIGLU_ALPHA = 1.702
LOG2E = math.log2(math.e)
Q_SCALE = HEAD_DIM ** -0.5 * LOG2E
PROJ_TN = 512
MOE_TM = 256
VMEM_LIMIT = 56 * 1024 * 1024

T_AQ, T_AK, T_BQ, T_CQ, T_KV, T_AV, T_GATE = 0, 1, 2, 3, 4, 5, 6


def _cparams(sem):
    return pltpu.CompilerParams(dimension_semantics=sem, vmem_limit_bytes=VMEM_LIMIT)


def _rms(x, g):
    return x * lax.rsqrt(jnp.mean(x * x, axis=-1, keepdims=True) + EPS) * g


def _sigmoid(x):
    return 1.0 / (1.0 + jnp.exp(-x))


def _pow2_tile(cap, *sizes):
    t = cap
    while any(s % t for s in sizes):
        t //= 2
    return t


def _adaln_kernel(c_ref, w_ref, b_ref, o_ref):
    c = c_ref[...]
    a = c * _sigmoid(c)
    o_ref[0] = jnp.dot(a, w_ref[0], preferred_element_type=F32,
                       precision=lax.Precision.HIGHEST) + b_ref[0]


def _adaln(cc, ada_w, ada_b):
    depth, d, n = ada_w.shape
    rows = cc.shape[0]
    tn = _pow2_tile(1024, n // 6)
    return pl.pallas_call(
        _adaln_kernel,
        grid=(depth, n // tn),
        in_specs=[pl.BlockSpec((rows, d), lambda l, j: (0, 0)),
                  pl.BlockSpec((1, d, tn), lambda l, j: (l, 0, j)),
                  pl.BlockSpec((1, 1, tn), lambda l, j: (l, 0, j))],
        out_specs=pl.BlockSpec((1, rows, tn), lambda l, j: (l, 0, j)),
        out_shape=jax.ShapeDtypeStruct((depth, rows, n), F32),
        compiler_params=_cparams(("parallel", "parallel")),
        name="adaln",
    )(cc, ada_w, ada_b.reshape(depth, 1, n))


def _norm_mod_kernel(x_ref, g_ref, sc_ref, sh_ref, h_ref):
    h = _rms(x_ref[...], g_ref[...]) * (1.0 + sc_ref[0]) + sh_ref[0]
    h_ref[...] = h.astype(h_ref.dtype)


def _seg_map(tiles_per_batch, n_batch):
    return lambda i: (jnp.minimum(i // tiles_per_batch, n_batch), 0, 0)


def _norm_mod(xc, g, sc, sh, seq, n_batch):
    n, d = xc.shape
    tm = _pow2_tile(1024, seq, n - n_batch * seq)
    seg = _seg_map(seq // tm, n_batch)
    return pl.pallas_call(
        _norm_mod_kernel,
        grid=(n // tm,),
        in_specs=[pl.BlockSpec((tm, d), lambda i: (i, 0)),
                  pl.BlockSpec((1, d), lambda i: (0, 0)),
                  pl.BlockSpec((1, 1, d), seg),
                  pl.BlockSpec((1, 1, d), seg)],
        out_specs=pl.BlockSpec((tm, d), lambda i: (i, 0)),
        out_shape=jax.ShapeDtypeStruct((n, d), BF16),
        compiler_params=_cparams(("parallel",)),
        name="norm_mod",
    )(xc, g.reshape(1, d), sc, sh)


_ROPE_S, _ROPE, _NORMQ, _NORMK, _PLAIN = range(5)
_TILE_OPS = {T_AQ: [_ROPE_S] * 4, T_AK: [_ROPE] * 4, T_BQ: [_NORMQ] * 4, T_CQ: [_ROPE_S] * 4,
             T_KV: [_NORMK, _ROPE, _PLAIN, _PLAIN]}


def _proj_kernel(h_ref, w_ref, cos_ref, sa_ref, sb_ref, gq_ref, gk_ref, o_ref, acc_ref):
    j = pl.program_id(1)
    acc_ref[...] = jnp.dot(h_ref[...], w_ref[...], preferred_element_type=F32)

    def rope(x):
        return (x * cos_ref[...] + pltpu.roll(x, LANES - 16, 1) * sa_ref[...]
                + pltpu.roll(x, 16, 1) * sb_ref[...])

    def head_norm(x, g):
        lo = lax.broadcasted_iota(jnp.int32, x.shape, 1) < HEAD_DIM
        x2 = x * x
        s_lo = jnp.sum(jnp.where(lo, x2, 0.0), axis=1, keepdims=True)
        s_hi = jnp.sum(jnp.where(lo, 0.0, x2), axis=1, keepdims=True)
        ms = jnp.where(lo, s_lo, s_hi) * (1.0 / HEAD_DIM)
        return x * lax.rsqrt(ms + EPS) * g

    def chunk(op, x):
        if op == _ROPE_S:
            return rope(x) * Q_SCALE
        if op == _ROPE:
            return rope(x)
        if op == _NORMQ:
            return rope(head_norm(x, gq_ref[...])) * Q_SCALE
        if op == _NORMK:
            return rope(head_norm(x, gk_ref[...]))
        return x

    for t, ops in _TILE_OPS.items():
        @pl.when(j == t)
        def _(ops=ops):
            for c, op in enumerate(ops):
                cs = slice(c * LANES, (c + 1) * LANES)
                o_ref[:, cs] = chunk(op, acc_ref[:, cs]).astype(o_ref.dtype)

    @pl.when(j >= T_AV)
    def _():
        o_ref[...] = acc_ref[...].astype(o_ref.dtype)


def _proj(h, w, cos, sa, sb, gq, gk):
    n, d = h.shape
    cols = w.shape[1]
    tm = _pow2_tile(1024, n)
    row = lambda i, j: (i, 0)
    vec = lambda i, j: (0, 0)
    return pl.pallas_call(
        _proj_kernel,
        grid=(n // tm, cols // PROJ_TN),
        in_specs=[pl.BlockSpec((tm, d), row),
                  pl.BlockSpec((d, PROJ_TN), lambda i, j: (0, j)),
                  pl.BlockSpec((tm, LANES), row),
                  pl.BlockSpec((tm, LANES), row),
                  pl.BlockSpec((tm, LANES), row),
                  pl.BlockSpec((1, LANES), vec),
                  pl.BlockSpec((1, LANES), vec)],
        out_specs=pl.BlockSpec((tm, PROJ_TN), lambda i, j: (i, j)),
        out_shape=jax.ShapeDtypeStruct((n, cols), BF16),
        scratch_shapes=[pltpu.VMEM((tm, PROJ_TN), F32)],
        compiler_params=_cparams(("parallel", "arbitrary")),
        name="proj",
    )(h, w, cos, sa, sb, gq, gk)


def _lane_lo(shape):
    return lax.broadcasted_iota(jnp.int32, shape, 1) < HEAD_DIM


def _split_heads(qv):
    lo = _lane_lo(qv.shape)
    zero = jnp.zeros_like(qv)
    return jnp.concatenate([jnp.where(lo, qv, zero), jnp.where(lo, zero, qv)], axis=0)


def _qk(lhs, k):
    return lax.dot_general(lhs, k, (((1,), (1,)), ((), ())), preferred_element_type=F32)


def _online_step(carry, lhs, k, v):
    m, l, acc = carry
    s = _qk(lhs, k)
    m_new = jnp.maximum(m, jnp.max(s, axis=1, keepdims=True))
    alpha = jnp.exp2(m - m_new)
    p = jnp.exp2((s - m_new).astype(BF16))
    l = alpha * l + jnp.sum(p.astype(F32), axis=1, keepdims=True)
    acc = alpha * acc + jnp.dot(p, v, preferred_element_type=F32)
    return m_new, l, acc


def _flash(lhs, segs, tk):
    rows = lhs.shape[0]
    carry = (jnp.full((rows, 1), NEG_INF, F32), jnp.zeros((rows, 1), F32), jnp.zeros((rows, LANES), F32))
    for k_ref, v_ref in segs:
        nk = k_ref.shape[0]
        t = min(tk, nk)
        for i in range(nk // t):
            carry = _online_step(carry, lhs, k_ref[i * t:(i + 1) * t, :], v_ref[i * t:(i + 1) * t, :])
    _, l, acc = carry
    return acc / l


def _attn_a_kernel(lam_init, n_seg, tk, q_ref, dl_ref, sg_ref, *refs):
    segs = [(refs[2 * i], refs[2 * i + 1]) for i in range(n_seg)]
    o_ref = refs[2 * n_seg]
    tq = q_ref.shape[0]
    o = _flash(_split_heads(q_ref[...]), segs, tk)
    dl = dl_ref[...]
    lam = (jnp.exp(jnp.sum(dl[0:1] * dl[1:2], axis=1, keepdims=True))
           - jnp.exp(jnp.sum(dl[2:3] * dl[3:4], axis=1, keepdims=True)) + lam_init)
    y = o[:tq] - lam * o[tq:]
    o_ref[...] = (_rms(y, sg_ref[...]) * (1.0 - lam_init)).astype(o_ref.dtype)


def _gqa_lhs(q_ref):
    return jnp.concatenate([_split_heads(q_ref[:, j * LANES:(j + 1) * LANES]) for j in range(Q_GROUP)], axis=0)


def _gqa_store(o, o_ref):
    tq = o_ref.shape[0]
    lo = _lane_lo((tq, LANES))
    for j in range(Q_GROUP):
        a = o[(2 * j) * tq:(2 * j + 1) * tq]
        b = o[(2 * j + 1) * tq:(2 * j + 2) * tq]
        o_ref[:, j * LANES:(j + 1) * LANES] = jnp.where(lo, a, b).astype(o_ref.dtype)


def _attn_b_kernel(n_seg, tk, q_ref, *refs):
    segs = [(refs[2 * i], refs[2 * i + 1]) for i in range(n_seg)]
    o_ref = refs[2 * n_seg]
    _gqa_store(_flash(_gqa_lhs(q_ref), segs, tk), o_ref)


def _attn_c_kernel(span, has_lat, sink_ref, q_ref, *refs):
    if has_lat:
        kl_ref, vl_ref, kc_ref, vc_ref, o_ref = refs
    else:
        kc_ref, vc_ref, o_ref = refs
    tq = q_ref.shape[0]
    lhs = _gqa_lhs(q_ref)
    rows = lhs.shape[0]
    sink = jnp.concatenate(
        [jnp.full((tq, 1), sink_ref[n * Q_GROUP + j] * LOG2E, F32) for j in range(Q_GROUP) for n in range(KV_HEADS)],
        axis=0)
    kc = kc_ref[...]
    vc = vc_ref[...]
    s_c = _qk(lhs, kc)
    m = jnp.maximum(jnp.max(s_c, axis=1, keepdims=True), sink)
    if has_lat:
        n_lat = kl_ref.shape[0]
        q0 = pl.program_id(1) * tq
        start = pl.multiple_of(jnp.clip(q0 - WINDOW, 0, n_lat - span), WINDOW)
        kw = kl_ref[pl.ds(start, span), :]
        vw = vl_ref[pl.ds(start, span), :]
        s_w = _qk(lhs, kw)
        qi = lax.broadcasted_iota(jnp.int32, (rows, span), 0) & (tq - 1)
        ki = lax.broadcasted_iota(jnp.int32, (rows, span), 1)
        dist = qi - ki + (q0 - start)
        s_w = jnp.where(jnp.abs(dist) <= WINDOW, s_w, NEG_INF)
        m = jnp.maximum(m, jnp.max(s_w, axis=1, keepdims=True))
    p_c = jnp.exp2((s_c - m).astype(BF16))
    l = jnp.sum(p_c.astype(F32), axis=1, keepdims=True) + jnp.exp2(sink - m)
    acc = jnp.dot(p_c, vc, preferred_element_type=F32)
    if has_lat:
        p_w = jnp.exp2((s_w - m).astype(BF16))
        l = l + jnp.sum(p_w.astype(F32), axis=1, keepdims=True)
        acc = acc + jnp.dot(p_w, vw, preferred_element_type=F32)
    _gqa_store(acc / l, o_ref)


def _mixers(p, seq, ctx_len, n_batch, lam_init, diff_lambda, subln, sink_perm, ctx_queries):
    n_lat = n_batch * seq
    cb = PROJ_TN // LANES
    if ctx_queries:
        tq, nq = ctx_len, 1
        tqa, nqa = tq, nq
        q_row = lambda b, i: n_lat // ctx_len + b
        qa_row = q_row
    else:
        tq = _pow2_tile(128, seq)
        nq = seq // tq
        q_row = lambda b, i: b * nq + i
        tqa = _pow2_tile(512, seq)
        nqa = seq // tqa
        qa_row = lambda b, i: b * nqa + i
    out_rows = n_batch * nq * tq
    tk = 512

    def kv_specs(col, grid_rank):
        def wrap(f):
            return (lambda b, h, i: f(b, h)) if grid_rank == 3 else (lambda b, i: f(b, 0))
        specs = []
        if not ctx_queries:
            specs.append(pl.BlockSpec((seq, LANES), wrap(lambda b, h: (b, col(h)))))
        specs.append(pl.BlockSpec((ctx_len, LANES), wrap(lambda b, h: (n_lat // ctx_len + b, col(h)))))
        return specs

    n_seg = 1 if ctx_queries else 2

    ka = kv_specs(lambda h: T_AK * cb + h, 3)
    va = kv_specs(lambda h: T_AV * cb + h, 3)
    ya = pl.pallas_call(
        functools.partial(_attn_a_kernel, lam_init, n_seg, tk),
        grid=(n_batch, A_HEADS, nqa),
        in_specs=[pl.BlockSpec((tqa, LANES), lambda b, h, i: (qa_row(b, i), T_AQ * cb + h)),
                  pl.BlockSpec(diff_lambda.shape, lambda b, h, i: (0, 0)),
                  pl.BlockSpec((1, LANES), lambda b, h, i: (0, 0))]
        + [s for kv in zip(ka, va) for s in kv],
        out_specs=pl.BlockSpec((tqa, LANES), lambda b, h, i: (b * nqa + i, h)),
        out_shape=jax.ShapeDtypeStruct((out_rows, A_HEADS * LANES), BF16),
        compiler_params=_cparams(("parallel", "parallel", "arbitrary")),
        name="attn_a_ctx" if ctx_queries else "attn_a",
    )(p, diff_lambda, subln.reshape(1, LANES), *([p] * (2 * n_seg)))

    kb = kv_specs(lambda h: T_KV * cb + 0, 2)
    vb = kv_specs(lambda h: T_KV * cb + 2, 2)
    yb = pl.pallas_call(
        functools.partial(_attn_b_kernel, n_seg, tk),
        grid=(n_batch, nq),
        in_specs=[pl.BlockSpec((tq, PROJ_TN), lambda b, i: (q_row(b, i), T_BQ))]
        + [s for kv in zip(kb, vb) for s in kv],
        out_specs=pl.BlockSpec((tq, PROJ_TN), lambda b, i: (b * nq + i, 0)),
        out_shape=jax.ShapeDtypeStruct((out_rows, PROJ_TN), BF16),
        compiler_params=_cparams(("parallel", "arbitrary")),
        name="attn_b_ctx" if ctx_queries else "attn_b",
    )(p, *([p] * (2 * n_seg)))

    kc = kv_specs(lambda h: T_KV * cb + 1, 2)
    vc = kv_specs(lambda h: T_KV * cb + 3, 2)
    span = min(tq + 2 * WINDOW, seq)
    yc = pl.pallas_call(
        functools.partial(_attn_c_kernel, span, not ctx_queries),
        grid=(n_batch, nq),
        in_specs=[pl.BlockSpec(memory_space=pltpu.SMEM),
                  pl.BlockSpec((tq, PROJ_TN), lambda b, i: (q_row(b, i), T_CQ))]
        + [s for kv in zip(kc, vc) for s in kv],
        out_specs=pl.BlockSpec((tq, PROJ_TN), lambda b, i: (b * nq + i, 0)),
        out_shape=jax.ShapeDtypeStruct((out_rows, PROJ_TN), BF16),
        compiler_params=_cparams(("parallel", "arbitrary")),
        name="attn_c_ctx" if ctx_queries else "attn_c",
    )(sink_perm, p, *([p] * (2 * n_seg)))
    return ya, yb, yc


def _merge_kernel(ya_ref, yb_ref, yc_ref, g0_ref, g1_ref, g2_ref, wb_ref, wo_ref, x_ref, gm_ref, ng1_ref,
                  ng2_ref, scf_ref, shf_ref, rw_ref, rb_ref, xo_ref, ho_ref, lo_ref):
    mix = None
    for y_ref, g_ref, i in ((ya_ref, g0_ref, 0), (yb_ref, g1_ref, 1), (yc_ref, g2_ref, 2)):
        t = _sigmoid(g_ref[...].astype(F32)) * jnp.dot(y_ref[...], wb_ref[i], preferred_element_type=F32)
        mix = t if mix is None else mix + t
    y = jnp.dot(mix.astype(BF16), wo_ref[...], preferred_element_type=F32)
    xn = x_ref[...] + gm_ref[0] * _rms(y, ng1_ref[...])
    xo_ref[...] = xn
    h = _rms(xn, ng2_ref[...]) * (1.0 + scf_ref[0]) + shf_ref[0]
    ho_ref[...] = h.astype(ho_ref.dtype)
    lo_ref[...] = jnp.dot(h, rw_ref[...], preferred_element_type=F32,
                          precision=lax.Precision.HIGHEST) + rb_ref[...]


def _merge(ya, yb, yc, p, wb, wo, xc, gm, ng1, ng2, scf, shf, rw, rb, seq, n_batch):
    n = ya.shape[0]
    d = xc.shape[1]
    tm = _pow2_tile(512, seq, n - n_batch * seq)
    seg = _seg_map(seq // tm, n_batch)
    row = lambda i: (i, 0)
    vec = lambda i: (0, 0)
    gate0 = T_GATE * PROJ_TN // d
    ne = rw.shape[1]
    return pl.pallas_call(
        _merge_kernel,
        grid=(n // tm,),
        in_specs=[pl.BlockSpec((tm, PROJ_TN), row)] * 3
        + [pl.BlockSpec((tm, d), lambda i, k=k: (i, gate0 + k)) for k in range(3)]
        + [pl.BlockSpec(wb.shape, lambda i: (0, 0, 0)),
           pl.BlockSpec(wo.shape, vec),
           pl.BlockSpec((tm, d), row),
           pl.BlockSpec((1, 1, d), seg),
           pl.BlockSpec((1, d), vec),
           pl.BlockSpec((1, d), vec),
           pl.BlockSpec((1, 1, d), seg),
           pl.BlockSpec((1, 1, d), seg),
           pl.BlockSpec(rw.shape, vec),
           pl.BlockSpec((1, ne), vec)],
        out_specs=[pl.BlockSpec((tm, d), row), pl.BlockSpec((tm, d), row), pl.BlockSpec((tm, ne), row)],
        out_shape=[jax.ShapeDtypeStruct((n, d), F32), jax.ShapeDtypeStruct((n, d), F32),
                   jax.ShapeDtypeStruct((n, ne), F32)],
        compiler_params=_cparams(("parallel",)),
        name="merge",
    )(ya, yb, yc, p, p, p, wb, wo, xc, gm, ng1.reshape(1, d), ng2.reshape(1, d), scf, shf, rw, rb)


def _row_copies(n_rows, make):
    def start():
        for r in range(n_rows):
            make(r).start()

    def wait():
        for r in range(n_rows):
            make(r).wait()
    return start, wait


def _moe_kernel(be_ref, nu_ref, tok_ref, tok_next_ref, dst_ref, h_hbm, w1_ref, b1_ref, w2_ref, b2_ref, cw_ref,
                u_hbm, xbuf, ybuf, gsem, ssem):
    i = pl.program_id(0)
    n_used = nu_ref[0]
    tm = xbuf.shape[1]
    slot = i % 2

    def gather(idx_ref, s):
        return _row_copies(
            tm, lambda r: pltpu.make_async_copy(h_hbm.at[idx_ref[0, 0, r]], xbuf.at[s, r], gsem.at[s]))

    def scatter(s):
        return _row_copies(
            tm, lambda r: pltpu.make_async_copy(ybuf.at[s, r], u_hbm.at[dst_ref[0, 0, r]], ssem.at[s]))

    @pl.when(i < n_used)
    def _():
        @pl.when(i == 0)
        def _():
            gather(tok_ref, slot)[0]()
            ybuf[1] = jnp.zeros(ybuf.shape[1:], ybuf.dtype)
            n_slots = u_hbm.shape[0] - 2 * tm
            for half in range(2):
                fill = pltpu.make_async_copy(ybuf.at[1], u_hbm.at[pl.ds(n_slots + half * tm, tm)], ssem.at[1])
                fill.start()
                fill.wait()

        @pl.when(i + 1 < n_used)
        def _():
            gather(tok_next_ref, 1 - slot)[0]()

        gather(tok_ref, slot)[1]()

        @pl.when(i >= 2)
        def _():
            scatter(slot)[1]()

        f = w2_ref.shape[1]
        x = xbuf[slot].astype(BF16)
        gu = jnp.dot(x, w1_ref[0], preferred_element_type=F32) + b1_ref[0]
        gate = jnp.minimum(gu[:, :f], SWIGLU_LIMIT)
        up = jnp.clip(gu[:, f:], -SWIGLU_LIMIT, SWIGLU_LIMIT)
        act = (up + 1.0) * (gate * _sigmoid(SWIGLU_ALPHA * gate))
        y = jnp.dot(act.astype(BF16), w2_ref[0], preferred_element_type=F32) + b2_ref[0]
        ybuf[slot] = y * cw_ref[0]
        scatter(slot)[0]()

        @pl.when(i == n_used - 1)
        def _():
            @pl.when(i >= 1)
            def _():
                scatter(1 - slot)[1]()
            scatter(slot)[1]()


def _moe_blocks(h, row_tok, row_dst, row_w, block_e, n_used, w1, b1, w2, b2, n_out):
    d = h.shape[1]
    ne, _, f2 = w1.shape
    f = w2.shape[1]
    nb = row_tok.shape[0]
    blk = lambda i, be, nu: (jnp.minimum(i, nu[0] - 1), 0, 0)
    blk_next = lambda i, be, nu: (jnp.minimum(i + 1, nu[0] - 1), 0, 0)
    wexp = lambda i, be, nu: (be[i], 0, 0)
    smem_blk = lambda m: pl.BlockSpec((1, 1, MOE_TM), m, memory_space=pltpu.SMEM)
    return pl.pallas_call(
        _moe_kernel,
        grid_spec=pltpu.PrefetchScalarGridSpec(
            num_scalar_prefetch=2,
            grid=(nb,),
            in_specs=[smem_blk(blk), smem_blk(blk_next), smem_blk(blk),
                      pl.BlockSpec(memory_space=pl.ANY),
                      pl.BlockSpec((1, d, f2), wexp),
                      pl.BlockSpec((1, 1, f2), wexp),
                      pl.BlockSpec((1, f, d), wexp),
                      pl.BlockSpec((1, 1, d), wexp),
                      pl.BlockSpec((1, MOE_TM, 1), blk)],
            out_specs=pl.BlockSpec(memory_space=pl.ANY),
            scratch_shapes=[pltpu.VMEM((2, MOE_TM, d), F32), pltpu.VMEM((2, MOE_TM, d), F32),
                            pltpu.SemaphoreType.DMA((2,)), pltpu.SemaphoreType.DMA((2,))],
        ),
        out_shape=jax.ShapeDtypeStruct((n_out, d), F32),
        compiler_params=_cparams(("arbitrary",)),
        name="moe",
    )(block_e, n_used, row_tok, row_tok, row_dst, h, w1, b1.reshape(ne, 1, f2), w2, b2.reshape(ne, 1, d), row_w)


def _moe(h, logits, w1, b1, w2, b2):
    n, d = h.shape
    top_val, top_idx = lax.top_k(logits, TOP_K)
    comb = jax.nn.softmax(top_val, axis=-1)
    n_slots = n * TOP_K
    flat_e = top_idx.reshape(-1).astype(jnp.int32)
    order = jnp.argsort(flat_e).astype(jnp.int32)
    counts = jnp.sum(flat_e[:, None] == jnp.arange(N_EXPERTS, dtype=jnp.int32), axis=0, dtype=jnp.int32)
    padded = (counts + MOE_TM - 1) // MOE_TM * MOE_TM
    start = jnp.cumsum(counts) - counts
    pad_end = jnp.cumsum(padded)
    pad_start = pad_end - padded
    nb = -(-n_slots // MOE_TM) + N_EXPERTS
    rows = nb * MOE_TM
    block_e = jnp.minimum(jnp.searchsorted(pad_end, jnp.arange(nb, dtype=jnp.int32) * MOE_TM, side='right'),
                          N_EXPERTS - 1).astype(jnp.int32)
    n_used = (pad_end[-1:] // MOE_TM).astype(jnp.int32)
    row_e = jnp.repeat(block_e, MOE_TM)
    row = jnp.arange(rows, dtype=jnp.int32)
    row_j = row - pad_start[row_e]
    row_valid = row_j < counts[row_e]
    row_slot = order[jnp.clip(start[row_e] + row_j, 0, n_slots - 1)]
    row_tok = jnp.where(row_valid, row_slot // TOP_K, 0)
    row_w = jnp.where(row_valid, comb.reshape(-1)[row_slot], 0.0)
    row_dst = jnp.where(row_valid, row_slot, n_slots + row % (2 * MOE_TM))
    shape3 = (nb, 1, MOE_TM)
    return _moe_blocks(h, row_tok.reshape(shape3), row_dst.reshape(shape3), row_w.reshape(nb, MOE_TM, 1), block_e,
                       n_used, w1, b1, w2, b2, n_slots + 2 * MOE_TM)


def _resid_kernel(with_next, x_ref, u_ref, gf_ref, ng3_ref, *refs):
    d = x_ref.shape[1]
    y = u_ref[:, 0:d]
    for k in range(1, TOP_K):
        y = y + u_ref[:, k * d:(k + 1) * d]
    xn = x_ref[...] + gf_ref[0] * _rms(y, ng3_ref[...])
    if with_next:
        ng0_ref, sc_ref, sh_ref, xo_ref, ho_ref = refs
        ho_ref[...] = (_rms(xn, ng0_ref[...]) * (1.0 + sc_ref[0]) + sh_ref[0]).astype(ho_ref.dtype)
    else:
        (xo_ref,) = refs
    xo_ref[...] = xn


def _resid(xc, u, gf, ng3, nxt, seq, n_batch):
    n, d = xc.shape
    u = u.reshape(-1, TOP_K * d)
    tm = _pow2_tile(256, seq, n - n_batch * seq)
    seg = _seg_map(seq // tm, n_batch)
    row = lambda i: (i, 0)
    vec = lambda i: (0, 0)
    in_specs = [pl.BlockSpec((tm, d), row), pl.BlockSpec((tm, TOP_K * d), row), pl.BlockSpec((1, 1, d), seg),
                pl.BlockSpec((1, d), vec)]
    args = [xc, u, gf, ng3.reshape(1, d)]
    out_specs = [pl.BlockSpec((tm, d), row)]
    out_shape = [jax.ShapeDtypeStruct((n, d), F32)]
    if nxt is not None:
        ng0, sc, sh = nxt
        in_specs += [pl.BlockSpec((1, d), vec), pl.BlockSpec((1, 1, d), seg), pl.BlockSpec((1, 1, d), seg)]
        args += [ng0.reshape(1, d), sc, sh]
        out_specs.append(pl.BlockSpec((tm, d), row))
        out_shape.append(jax.ShapeDtypeStruct((n, d), BF16))
    return pl.pallas_call(
        functools.partial(_resid_kernel, nxt is not None),
        grid=(n // tm,),
        in_specs=in_specs, out_specs=out_specs, out_shape=out_shape,
        compiler_params=_cparams(("parallel",)),
        name="resid",
    )(*args)


def _gqa_perm():
    idx = np.arange(KV_HEADS * Q_GROUP * HEAD_DIM).reshape(KV_HEADS, Q_GROUP, HEAD_DIM)
    return idx.transpose(1, 0, 2).reshape(-1)


def _w_in_perm():
    a_qk = A_HEADS * 2 * HEAD_DIM
    sizes = dict(aq=a_qk, ak=a_qk, av=A_HEADS * 2 * HEAD_DIM, bq=512, bk=128, bv=128, cq=512, ck=128, cv=128)
    off, o = {}, 0
    for k in ("aq", "ak", "av", "bq", "bk", "bv", "cq", "ck", "cv"):
        off[k] = o
        o += sizes[k]
    seg = lambda k: off[k] + np.arange(sizes[k])
    g = _gqa_perm()
    return np.concatenate([seg("aq"), seg("ak"), off["bq"] + g, off["cq"] + g,
                           seg("bk"), seg("ck"), seg("bv"), seg("cv"), seg("av")]), o


def _rope_tables(seq, n_ctx_rows):
    rows = seq // GRID_W
    row = jnp.repeat(jnp.arange(rows, dtype=F32), GRID_W)
    col = jnp.tile(jnp.arange(GRID_W, dtype=F32), rows)
    inv_freq = ROPE_THETA ** (-jnp.arange(AXIS_PAIRS, dtype=F32) / AXIS_PAIRS)
    ang_r = row[:, None] * inv_freq
    ang_c = col[:, None] * inv_freq
    ang = jnp.concatenate([ang_r, ang_r, ang_c, ang_c], axis=-1)
    cos, sin = jnp.cos(ang), jnp.sin(ang)
    first_half = (np.arange(HEAD_DIM) % (2 * AXIS_PAIRS)) < AXIS_PAIRS
    sa = jnp.where(first_half, -sin, 0.0)
    sb = jnp.where(first_half, 0.0, sin)
    return [jnp.tile(t, (1, LANES // HEAD_DIM)) for t in (cos, sa, sb)]


def kernel(x, c, ctx, c_ctx, ada_w, ada_b, norm_g, w_in, diff_lambda, diff_subln, qk_norm, sink, w_branch, w_out,
           router_w, router_b, w1, b1, w2, b2):
    n_batch, seq, d = x.shape
    ctx_len = ctx.shape[1]
    depth = ada_w.shape[0]
    n_lat, n_ctx = n_batch * seq, n_batch * ctx_len

    mod_rows = -(-(n_batch + 1) // 8) * 8
    cc = jnp.zeros((mod_rows, d), F32).at[:n_batch].set(c).at[n_batch].set(c_ctx)
    mod = _adaln(cc, ada_w, ada_b).reshape(depth, mod_rows, 6, 1, d).transpose(0, 2, 1, 3, 4)

    perm, n_qkv = _w_in_perm()
    perm = np.concatenate([perm, n_qkv + np.arange(w_in.shape[2] - n_qkv)])
    gperm = _gqa_perm()
    cos_l, sa_l, sb_l = _rope_tables(seq, n_ctx)
    ones = jnp.ones((n_ctx, LANES), F32)
    zeros = jnp.zeros((n_ctx, LANES), F32)
    cos_t = jnp.concatenate([jnp.tile(cos_l, (n_batch, 1)), ones])
    sa_t = jnp.concatenate([jnp.tile(sa_l, (n_batch, 1)), zeros])
    sb_t = jnp.concatenate([jnp.tile(sb_l, (n_batch, 1)), zeros])
    ne_pad = -(-N_EXPERTS // LANES) * LANES

    xc = jnp.concatenate([x.reshape(n_lat, d), ctx.reshape(n_ctx, d)])
    sh_m, sc_m = mod[0, 0], mod[0, 1]
    h = _norm_mod(xc, norm_g[0, 0], sc_m, sh_m, seq, n_batch)

    for l in range(depth):
        last = l == depth - 1
        lam_init = 0.8 - 0.6 * math.exp(-0.3 * l)
        sh_m, sc_m, g_m, sh_f, sc_f, g_f = (mod[l, k] for k in range(6))
        w_l = w_in[l][:, perm].astype(BF16)
        gq = jnp.tile(qk_norm[l, 0], LANES // HEAD_DIM).reshape(1, LANES)
        gk = jnp.tile(qk_norm[l, 1], LANES // HEAD_DIM).reshape(1, LANES)
        p = _proj(h, w_l, cos_t, sa_t, sb_t, gq, gk)

        ya, yb, yc = _mixers(p, seq, ctx_len, n_batch, lam_init, diff_lambda[l], diff_subln[l], sink[l], False)
        if not last:
            ya_c, yb_c, yc_c = _mixers(p, seq, ctx_len, n_batch, lam_init, diff_lambda[l], diff_subln[l], sink[l],
                                       True)
            ya, yb, yc = (jnp.concatenate(t) for t in ((ya, ya_c), (yb, yb_c), (yc, yc_c)))
        n_rows = ya.shape[0]

        wb = jnp.stack([w_branch[l, 0], w_branch[l, 1][gperm], w_branch[l, 2][gperm]]).astype(BF16)
        rw = jnp.zeros((d, ne_pad), F32).at[:, :N_EXPERTS].set(router_w[l])
        rb = jnp.zeros((1, ne_pad), F32).at[0, :N_EXPERTS].set(router_b[l])
        xc, h_f, logits = _merge(ya, yb, yc, p, wb, w_out[l].astype(BF16), xc[:n_rows], g_m, norm_g[l, 1],
                                 norm_g[l, 2], sc_f, sh_f, rw, rb, seq, n_batch)

        u = _moe(h_f, logits[:, :N_EXPERTS], w1[l].astype(BF16), b1[l], w2[l].astype(BF16), b2[l])
        if last:
            (xc,) = _resid(xc, u, g_f, norm_g[l, 3], None, seq, n_batch)
        else:
            xc, h = _resid(xc, u, g_f, norm_g[l, 3], (norm_g[l + 1, 0], mod[l + 1, 1], mod[l + 1, 0]), seq,
                           n_batch)
    return xc[:n_lat].reshape(n_batch, seq, d)
```

```python
import functools
import math

import jax
import jax.numpy as jnp
import numpy as np
from jax import lax
from jax.experimental import pallas as pl
from jax.experimental.pallas import tpu as pltpu

F32 = jnp.float32
BF16 = jnp.bfloat16

HEAD_DIM = 64
LANES = 128
GRID_W = 64
AXIS_PAIRS = HEAD_DIM // 4
ROPE_THETA = 10000.0
EPS = 1e-6
NEG_INF = -1e30
WINDOW = 128
A_HEADS = 4
KV_HEADS = 2
Q_GROUP = 4
N_EXPERTS = 32
TOP_K = 4
TOP_K_SHIFT = 2
SWIGLU_LIMIT = 7.0
SWIGLU_ALPHA = 1.702
LOG2E = math.log2(math.e)
Q_SCALE = HEAD_DIM ** -0.5 * LOG2E
PROJ_TN = 512
MOE_TM = 256
VMEM_LIMIT = 56 * 1024 * 1024

T_AQ, T_AK, T_BQ, T_CQ, T_KV, T_AV, T_GATE = 0, 1, 2, 3, 4, 5, 6


def _cparams(sem):
    return pltpu.CompilerParams(dimension_semantics=sem, vmem_limit_bytes=VMEM_LIMIT)


def _rms(x, g):
    return x * lax.rsqrt(jnp.mean(x * x, axis=-1, keepdims=True) + EPS) * g


def _sigmoid(x):
    return 1.0 / (1.0 + jnp.exp(-x))


def _pow2_tile(cap, *sizes):
    t = cap
    while any(s % t for s in sizes):
        t //= 2
    return t


def _adaln_kernel(c_ref, w_ref, b_ref, o_ref):
    c = c_ref[...]
    a = c * _sigmoid(c)
    o_ref[0] = jnp.dot(a, w_ref[0], preferred_element_type=F32,
                       precision=lax.Precision.HIGHEST) + b_ref[0]


def _adaln(cc, ada_w, ada_b):
    depth, d, n = ada_w.shape
    rows = cc.shape[0]
    tn = _pow2_tile(1024, n // 6)
    return pl.pallas_call(
        _adaln_kernel,
        grid=(depth, n // tn),
        in_specs=[pl.BlockSpec((rows, d), lambda l, j: (0, 0)),
                  pl.BlockSpec((1, d, tn), lambda l, j: (l, 0, j)),
                  pl.BlockSpec((1, 1, tn), lambda l, j: (l, 0, j))],
        out_specs=pl.BlockSpec((1, rows, tn), lambda l, j: (l, 0, j)),
        out_shape=jax.ShapeDtypeStruct((depth, rows, n), F32),
        compiler_params=_cparams(("parallel", "parallel")),
        name="adaln",
    )(cc, ada_w, ada_b.reshape(depth, 1, n))


def _norm_mod_kernel(x_ref, g_ref, sc_ref, sh_ref, h_ref):
    h = _rms(x_ref[...], g_ref[...]) * (1.0 + sc_ref[0]) + sh_ref[0]
    h_ref[...] = h.astype(h_ref.dtype)


def _seg_map(tiles_per_batch, n_batch):
    return lambda i: (jnp.minimum(i // tiles_per_batch, n_batch), 0, 0)


def _norm_mod(xc, g, sc, sh, seq, n_batch):
    n, d = xc.shape
    tm = _pow2_tile(1024, seq, n - n_batch * seq)
    seg = _seg_map(seq // tm, n_batch)
    return pl.pallas_call(
        _norm_mod_kernel,
        grid=(n // tm,),
        in_specs=[pl.BlockSpec((tm, d), lambda i: (i, 0)),
                  pl.BlockSpec((1, d), lambda i: (0, 0)),
                  pl.BlockSpec((1, 1, d), seg),
                  pl.BlockSpec((1, 1, d), seg)],
        out_specs=pl.BlockSpec((tm, d), lambda i: (i, 0)),
        out_shape=jax.ShapeDtypeStruct((n, d), BF16),
        compiler_params=_cparams(("parallel",)),
        name="norm_mod",
    )(xc, g.reshape(1, d), sc, sh)


_ROPE_S, _ROPE, _NORMQ, _NORMK, _PLAIN = range(5)
_TILE_OPS = {T_AQ: [_ROPE_S] * 4, T_AK: [_ROPE] * 4, T_BQ: [_NORMQ] * 4, T_CQ: [_ROPE_S] * 4,
             T_KV: [_NORMK, _ROPE, _PLAIN, _PLAIN]}


def _proj_kernel(h_ref, w_ref, cos_ref, sa_ref, sb_ref, gq_ref, gk_ref, o_ref, acc_ref):
    j = pl.program_id(1)
    acc_ref[...] = jnp.dot(h_ref[...], w_ref[...], preferred_element_type=F32)

    def rope(x):
        return (x * cos_ref[...] + pltpu.roll(x, LANES - 16, 1) * sa_ref[...]
                + pltpu.roll(x, 16, 1) * sb_ref[...])

    def head_norm(x, g):
        lo = lax.broadcasted_iota(jnp.int32, x.shape, 1) < HEAD_DIM
        x2 = x * x
        s_lo = jnp.sum(jnp.where(lo, x2, 0.0), axis=1, keepdims=True)
        s_hi = jnp.sum(jnp.where(lo, 0.0, x2), axis=1, keepdims=True)
        ms = jnp.where(lo, s_lo, s_hi) * (1.0 / HEAD_DIM)
        return x * lax.rsqrt(ms + EPS) * g

    def chunk(op, x):
        if op == _ROPE_S:
            return rope(x) * Q_SCALE
        if op == _ROPE:
            return rope(x)
        if op == _NORMQ:
            return rope(head_norm(x, gq_ref[...])) * Q_SCALE
        if op == _NORMK:
            return rope(head_norm(x, gk_ref[...]))
        return x

    for t, ops in _TILE_OPS.items():
        @pl.when(j == t)
        def _(ops=ops):
            for c, op in enumerate(ops):
                cs = slice(c * LANES, (c + 1) * LANES)
                o_ref[:, cs] = chunk(op, acc_ref[:, cs]).astype(o_ref.dtype)

    @pl.when(j >= T_AV)
    def _():
        o_ref[...] = acc_ref[...].astype(o_ref.dtype)


def _proj(h, w, cos, sa, sb, gq, gk):
    n, d = h.shape
    cols = w.shape[1]
    tm = _pow2_tile(1024, n)
    row = lambda i, j: (i, 0)
    vec = lambda i, j: (0, 0)
    return pl.pallas_call(
        _proj_kernel,
        grid=(n // tm, cols // PROJ_TN),
        in_specs=[pl.BlockSpec((tm, d), row),
                  pl.BlockSpec((d, PROJ_TN), lambda i, j: (0, j)),
                  pl.BlockSpec((tm, LANES), row),
                  pl.BlockSpec((tm, LANES), row),
                  pl.BlockSpec((tm, LANES), row),
                  pl.BlockSpec((1, LANES), vec),
                  pl.BlockSpec((1, LANES), vec)],
        out_specs=pl.BlockSpec((tm, PROJ_TN), lambda i, j: (i, j)),
        out_shape=jax.ShapeDtypeStruct((n, cols), BF16),
        scratch_shapes=[pltpu.VMEM((tm, PROJ_TN), F32)],
        compiler_params=_cparams(("parallel", "arbitrary")),
        name="proj",
    )(h, w, cos, sa, sb, gq, gk)


def _lane_lo(shape):
    return lax.broadcasted_iota(jnp.int32, shape, 1) < HEAD_DIM


def _split_heads(qv):
    lo = _lane_lo(qv.shape)
    zero = jnp.zeros_like(qv)
    return jnp.concatenate([jnp.where(lo, qv, zero), jnp.where(lo, zero, qv)], axis=0)


def _qk(lhs, k):
    return lax.dot_general(lhs, k, (((1,), (1,)), ((), ())), preferred_element_type=F32)


def _online_step(carry, lhs, k, v):
    m, l, acc = carry
    s = _qk(lhs, k)
    m_new = jnp.maximum(m, jnp.max(s, axis=1, keepdims=True))
    alpha = jnp.exp2(m - m_new)
    p = jnp.exp2((s - m_new).astype(BF16))
    l = alpha * l + jnp.sum(p.astype(F32), axis=1, keepdims=True)
    acc = alpha * acc + jnp.dot(p, v, preferred_element_type=F32)
    return m_new, l, acc


def _flash(lhs, segs, tk):
    rows = lhs.shape[0]
    carry = (jnp.full((rows, 1), NEG_INF, F32), jnp.zeros((rows, 1), F32), jnp.zeros((rows, LANES), F32))
    for k_ref, v_ref in segs:
        nk = k_ref.shape[0]
        t = min(tk, nk)
        for i in range(nk // t):
            carry = _online_step(carry, lhs, k_ref[i * t:(i + 1) * t, :], v_ref[i * t:(i + 1) * t, :])
    _, l, acc = carry
    return acc / l


def _attn_a_kernel(lam_init, n_seg, tk, q_ref, dl_ref, sg_ref, *refs):
    segs = [(refs[2 * i], refs[2 * i + 1]) for i in range(n_seg)]
    o_ref = refs[2 * n_seg]
    tq = q_ref.shape[0]
    o = _flash(_split_heads(q_ref[...]), segs, tk)
    dl = dl_ref[...]
    lam = (jnp.exp(jnp.sum(dl[0:1] * dl[1:2], axis=1, keepdims=True))
           - jnp.exp(jnp.sum(dl[2:3] * dl[3:4], axis=1, keepdims=True)) + lam_init)
    y = o[:tq] - lam * o[tq:]
    o_ref[...] = (_rms(y, sg_ref[...]) * (1.0 - lam_init)).astype(o_ref.dtype)


def _gqa_lhs(q_ref):
    return jnp.concatenate([_split_heads(q_ref[:, j * LANES:(j + 1) * LANES]) for j in range(Q_GROUP)], axis=0)


def _gqa_store(o, o_ref):
    tq = o_ref.shape[0]
    lo = _lane_lo((tq, LANES))
    for j in range(Q_GROUP):
        a = o[(2 * j) * tq:(2 * j + 1) * tq]
        b = o[(2 * j + 1) * tq:(2 * j + 2) * tq]
        o_ref[:, j * LANES:(j + 1) * LANES] = jnp.where(lo, a, b).astype(o_ref.dtype)


def _attn_b_kernel(n_seg, tk, q_ref, *refs):
    segs = [(refs[2 * i], refs[2 * i + 1]) for i in range(n_seg)]
    o_ref = refs[2 * n_seg]
    _gqa_store(_flash(_gqa_lhs(q_ref), segs, tk), o_ref)


def _attn_c_kernel(span, has_lat, sink_ref, q_ref, *refs):
    if has_lat:
        kl_ref, vl_ref, kc_ref, vc_ref, o_ref = refs
    else:
        kc_ref, vc_ref, o_ref = refs
    tq = q_ref.shape[0]
    lhs = _gqa_lhs(q_ref)
    rows = lhs.shape[0]
    sink = jnp.concatenate(
        [jnp.full((tq, 1), sink_ref[n * Q_GROUP + j] * LOG2E, F32) for j in range(Q_GROUP) for n in range(KV_HEADS)],
        axis=0)
    kc = kc_ref[...]
    vc = vc_ref[...]
    s_c = _qk(lhs, kc)
    m = jnp.maximum(jnp.max(s_c, axis=1, keepdims=True), sink)
    if has_lat:
        n_lat = kl_ref.shape[0]
        q0 = pl.program_id(1) * tq
        start = pl.multiple_of(jnp.clip(q0 - WINDOW, 0, n_lat - span), WINDOW)
        kw = kl_ref[pl.ds(start, span), :]
        vw = vl_ref[pl.ds(start, span), :]
        s_w = _qk(lhs, kw)
        qi = lax.broadcasted_iota(jnp.int32, (rows, span), 0) & (tq - 1)
        ki = lax.broadcasted_iota(jnp.int32, (rows, span), 1)
        dist = qi - ki + (q0 - start)
        s_w = jnp.where(jnp.abs(dist) <= WINDOW, s_w, NEG_INF)
        m = jnp.maximum(m, jnp.max(s_w, axis=1, keepdims=True))
    p_c = jnp.exp2((s_c - m).astype(BF16))
    l = jnp.sum(p_c.astype(F32), axis=1, keepdims=True) + jnp.exp2(sink - m)
    acc = jnp.dot(p_c, vc, preferred_element_type=F32)
    if has_lat:
        p_w = jnp.exp2((s_w - m).astype(BF16))
        l = l + jnp.sum(p_w.astype(F32), axis=1, keepdims=True)
        acc = acc + jnp.dot(p_w, vw, preferred_element_type=F32)
    _gqa_store(acc / l, o_ref)


def _mixers(p, seq, ctx_len, n_batch, lam_init, diff_lambda, subln, sink_perm, ctx_queries):
    n_lat = n_batch * seq
    cb = PROJ_TN // LANES
    if ctx_queries:
        tq, nq = ctx_len, 1
        tqa, nqa = tq, nq
        q_row = lambda b, i: n_lat // ctx_len + b
        qa_row = q_row
    else:
        tq = _pow2_tile(128, seq)
        nq = seq // tq
        q_row = lambda b, i: b * nq + i
        tqa = _pow2_tile(512, seq)
        nqa = seq // tqa
        qa_row = lambda b, i: b * nqa + i
    out_rows = n_batch * nq * tq
    tk = 512

    def kv_specs(col, grid_rank):
        def wrap(f):
            return (lambda b, h, i: f(b, h)) if grid_rank == 3 else (lambda b, i: f(b, 0))
        specs = []
        if not ctx_queries:
            specs.append(pl.BlockSpec((seq, LANES), wrap(lambda b, h: (b, col(h)))))
        specs.append(pl.BlockSpec((ctx_len, LANES), wrap(lambda b, h: (n_lat // ctx_len + b, col(h)))))
        return specs

    n_seg = 1 if ctx_queries else 2

    ka = kv_specs(lambda h: T_AK * cb + h, 3)
    va = kv_specs(lambda h: T_AV * cb + h, 3)
    ya = pl.pallas_call(
        functools.partial(_attn_a_kernel, lam_init, n_seg, tk),
        grid=(n_batch, A_HEADS, nqa),
        in_specs=[pl.BlockSpec((tqa, LANES), lambda b, h, i: (qa_row(b, i), T_AQ * cb + h)),
                  pl.BlockSpec(diff_lambda.shape, lambda b, h, i: (0, 0)),
                  pl.BlockSpec((1, LANES), lambda b, h, i: (0, 0))]
        + [s for kv in zip(ka, va) for s in kv],
        out_specs=pl.BlockSpec((tqa, LANES), lambda b, h, i: (b * nqa + i, h)),
        out_shape=jax.ShapeDtypeStruct((out_rows, A_HEADS * LANES), BF16),
        compiler_params=_cparams(("parallel", "parallel", "arbitrary")),
        name="attn_a_ctx" if ctx_queries else "attn_a",
    )(p, diff_lambda, subln.reshape(1, LANES), *([p] * (2 * n_seg)))

    kb = kv_specs(lambda h: T_KV * cb + 0, 2)
    vb = kv_specs(lambda h: T_KV * cb + 2, 2)
    yb = pl.pallas_call(
        functools.partial(_attn_b_kernel, n_seg, tk),
        grid=(n_batch, nq),
        in_specs=[pl.BlockSpec((tq, PROJ_TN), lambda b, i: (q_row(b, i), T_BQ))]
        + [s for kv in zip(kb, vb) for s in kv],
        out_specs=pl.BlockSpec((tq, PROJ_TN), lambda b, i: (b * nq + i, 0)),
        out_shape=jax.ShapeDtypeStruct((out_rows, PROJ_TN), BF16),
        compiler_params=_cparams(("parallel", "arbitrary")),
        name="attn_b_ctx" if ctx_queries else "attn_b",
    )(p, *([p] * (2 * n_seg)))

    kc = kv_specs(lambda h: T_KV * cb + 1, 2)
    vc = kv_specs(lambda h: T_KV * cb + 3, 2)
    span = min(tq + 2 * WINDOW, seq)
    yc = pl.pallas_call(
        functools.partial(_attn_c_kernel, span, not ctx_queries),
        grid=(n_batch, nq),
        in_specs=[pl.BlockSpec(memory_space=pltpu.SMEM),
                  pl.BlockSpec((tq, PROJ_TN), lambda b, i: (q_row(b, i), T_CQ))]
        + [s for kv in zip(kc, vc) for s in kv],
        out_specs=pl.BlockSpec((tq, PROJ_TN), lambda b, i: (b * nq + i, 0)),
        out_shape=jax.ShapeDtypeStruct((out_rows, PROJ_TN), BF16),
        compiler_params=_cparams(("parallel", "arbitrary")),
        name="attn_c_ctx" if ctx_queries else "attn_c",
    )(sink_perm, p, *([p] * (2 * n_seg)))
    return ya, yb, yc


def _merge_kernel(ya_ref, yb_ref, yc_ref, g0_ref, g1_ref, g2_ref, wb_ref, wo_ref, x_ref, gm_ref, ng1_ref,
                  ng2_ref, scf_ref, shf_ref, rw_ref, rb_ref, xo_ref, ho_ref, lo_ref):
    mix = None
    for y_ref, g_ref, i in ((ya_ref, g0_ref, 0), (yb_ref, g1_ref, 1), (yc_ref, g2_ref, 2)):
        t = _sigmoid(g_ref[...].astype(F32)) * jnp.dot(y_ref[...], wb_ref[i], preferred_element_type=F32)
        mix = t if mix is None else mix + t
    y = jnp.dot(mix.astype(BF16), wo_ref[...], preferred_element_type=F32)
    xn = x_ref[...] + gm_ref[0] * _rms(y, ng1_ref[...])
    xo_ref[...] = xn
    h = _rms(xn, ng2_ref[...]) * (1.0 + scf_ref[0]) + shf_ref[0]
    ho_ref[...] = h.astype(ho_ref.dtype)
    lo_ref[...] = jnp.dot(h, rw_ref[...], preferred_element_type=F32,
                          precision=lax.Precision.HIGHEST) + rb_ref[...]


def _merge(ya, yb, yc, p, wb, wo, xc, gm, ng1, ng2, scf, shf, rw, rb, seq, n_batch):
    n = ya.shape[0]
    d = xc.shape[1]
    tm = _pow2_tile(512, seq, n - n_batch * seq)
    seg = _seg_map(seq // tm, n_batch)
    row = lambda i: (i, 0)
    vec = lambda i: (0, 0)
    gate0 = T_GATE * PROJ_TN // d
    ne = rw.shape[1]
    return pl.pallas_call(
        _merge_kernel,
        grid=(n // tm,),
        in_specs=[pl.BlockSpec((tm, PROJ_TN), row)] * 3
        + [pl.BlockSpec((tm, d), lambda i, k=k: (i, gate0 + k)) for k in range(3)]
        + [pl.BlockSpec(wb.shape, lambda i: (0, 0, 0)),
           pl.BlockSpec(wo.shape, vec),
           pl.BlockSpec((tm, d), row),
           pl.BlockSpec((1, 1, d), seg),
           pl.BlockSpec((1, d), vec),
           pl.BlockSpec((1, d), vec),
           pl.BlockSpec((1, 1, d), seg),
           pl.BlockSpec((1, 1, d), seg),
           pl.BlockSpec(rw.shape, vec),
           pl.BlockSpec((1, ne), vec)],
        out_specs=[pl.BlockSpec((tm, d), row), pl.BlockSpec((tm, d), row), pl.BlockSpec((tm, ne), row)],
        out_shape=[jax.ShapeDtypeStruct((n, d), F32), jax.ShapeDtypeStruct((n, d), F32),
                   jax.ShapeDtypeStruct((n, ne), F32)],
        compiler_params=_cparams(("parallel",)),
        name="merge",
    )(ya, yb, yc, p, p, p, wb, wo, xc, gm, ng1.reshape(1, d), ng2.reshape(1, d), scf, shf, rw, rb)


def _row_copies(n_rows, make):
    def start():
        for r in range(n_rows):
            make(r).start()

    def wait():
        for r in range(n_rows):
            make(r).wait()
    return start, wait


def _moe_kernel(be_ref, nu_ref, tok_ref, tok_next_ref, drow_ref, dcol_ref, h_hbm, w1_ref, b1_ref, w2_ref, b2_ref, cw_ref,
                u_hbm, xbuf, ybuf, w1b, w2b, gsem, ssem):
    i = pl.program_id(0)
    n_used = nu_ref[0]
    tm, d = xbuf.shape[1:]
    slot = i % 2

    def gather(idx_ref, s):
        return _row_copies(
            tm, lambda r: pltpu.make_async_copy(h_hbm.at[idx_ref[0, 0, r]], xbuf.at[s, r], gsem.at[s]))

    def scatter(s):
        def make(r):
            col = pl.multiple_of(dcol_ref[0, 0, r], d)
            return pltpu.make_async_copy(ybuf.at[s, r], u_hbm.at[drow_ref[0, 0, r], pl.ds(col, d)], ssem.at[s])
        return _row_copies(tm, make)

    @pl.when(i < n_used)
    def _():
        @pl.when(i == 0)
        def _():
            gather(tok_ref, slot)[0]()
            ybuf[1] = jnp.zeros(ybuf.shape[1:], ybuf.dtype)
            tail = 2 * tm // TOP_K
            for k in range(TOP_K):
                fill = pltpu.make_async_copy(ybuf.at[1, pl.ds(0, tail)],
                                             u_hbm.at[pl.ds(u_hbm.shape[0] - tail, tail), pl.ds(k * d, d)], ssem.at[1])
                fill.start()
                fill.wait()

        @pl.when(i + 1 < n_used)
        def _():
            gather(tok_next_ref, 1 - slot)[0]()

        gather(tok_ref, slot)[1]()

        @pl.when(i >= 2)
        def _():
            scatter(slot)[1]()

        @pl.when((i == 0) | (be_ref[i] != be_ref[jnp.maximum(i - 1, 0)]))
        def _():
            w1b[...] = w1_ref[0].astype(BF16)
            w2b[...] = w2_ref[0].astype(BF16)

        f = w2_ref.shape[1]
        x = xbuf[slot].astype(BF16)
        gu = jnp.dot(x, w1b[...], preferred_element_type=F32) + b1_ref[0]
        gate = jnp.minimum(gu[:, :f], SWIGLU_LIMIT)
        up = jnp.clip(gu[:, f:], -SWIGLU_LIMIT, SWIGLU_LIMIT)
        act = (up + 1.0) * (gate * _sigmoid(SWIGLU_ALPHA * gate))
        y = jnp.dot(act.astype(BF16), w2b[...], preferred_element_type=F32) + b2_ref[0]
        ybuf[slot] = y * cw_ref[0]
        scatter(slot)[0]()

        @pl.when(i == n_used - 1)
        def _():
            @pl.when(i >= 1)
            def _():
                scatter(1 - slot)[1]()
            scatter(slot)[1]()


def _moe_blocks(h, row_tok, row_dst, row_w, block_e, n_used, w1, b1, w2, b2, n_out):
    d = h.shape[1]
    ne, _, f2 = w1.shape
    f = w2.shape[1]
    nb = row_tok.shape[0]
    blk = lambda i, be, nu: (jnp.minimum(i, nu[0] - 1), 0, 0)
    blk_next = lambda i, be, nu: (jnp.minimum(i + 1, nu[0] - 1), 0, 0)
    wexp = lambda i, be, nu: (be[i], 0, 0)
    smem_blk = lambda m: pl.BlockSpec((1, 1, MOE_TM), m, memory_space=pltpu.SMEM)
    return pl.pallas_call(
        _moe_kernel,
        grid_spec=pltpu.PrefetchScalarGridSpec(
            num_scalar_prefetch=2,
            grid=(nb,),
            in_specs=[smem_blk(blk), smem_blk(blk_next), smem_blk(blk), smem_blk(blk),
                      pl.BlockSpec(memory_space=pl.ANY),
                      pl.BlockSpec((1, d, f2), wexp),
                      pl.BlockSpec((1, 1, f2), wexp),
                      pl.BlockSpec((1, f, d), wexp),
                      pl.BlockSpec((1, 1, d), wexp),
                      pl.BlockSpec((1, MOE_TM, 1), blk)],
            out_specs=pl.BlockSpec(memory_space=pl.ANY),
            scratch_shapes=[pltpu.VMEM((2, MOE_TM, d), F32), pltpu.VMEM((2, MOE_TM, d), F32),
                            pltpu.VMEM((d, f2), BF16), pltpu.VMEM((f, d), BF16),
                            pltpu.SemaphoreType.DMA((2,)), pltpu.SemaphoreType.DMA((2,))],
        ),
        out_shape=jax.ShapeDtypeStruct((n_out // TOP_K, TOP_K * d), F32),
        compiler_params=_cparams(("arbitrary",)),
        name="moe",
    )(block_e, n_used, row_tok, row_tok, row_dst >> TOP_K_SHIFT, (row_dst & (TOP_K - 1)) * d, h, w1, b1.reshape(ne, 1, f2), w2, b2.reshape(ne, 1, d), row_w)


def _moe(h, logits, w1, b1, w2, b2):
    n, d = h.shape
    top_val, top_idx = lax.top_k(logits, TOP_K)
    comb = jax.nn.softmax(top_val, axis=-1)
    n_slots = n * TOP_K
    flat_e = top_idx.reshape(-1).astype(jnp.int32)
    order = jnp.argsort(flat_e).astype(jnp.int32)
    counts = jnp.sum(flat_e[:, None] == jnp.arange(N_EXPERTS, dtype=jnp.int32), axis=0, dtype=jnp.int32)
    padded = (counts + MOE_TM - 1) // MOE_TM * MOE_TM
    start = jnp.cumsum(counts) - counts
    pad_end = jnp.cumsum(padded)
    pad_start = pad_end - padded
    nb = -(-n_slots // MOE_TM) + N_EXPERTS
    rows = nb * MOE_TM
    block_e = jnp.minimum(jnp.searchsorted(pad_end, jnp.arange(nb, dtype=jnp.int32) * MOE_TM, side='right'),
                          N_EXPERTS - 1).astype(jnp.int32)
    n_used = (pad_end[-1:] // MOE_TM).astype(jnp.int32)
    row_e = jnp.repeat(block_e, MOE_TM)
    row = jnp.arange(rows, dtype=jnp.int32)
    row_j = row - pad_start[row_e]
    row_valid = row_j < counts[row_e]
    row_slot = order[jnp.clip(start[row_e] + row_j, 0, n_slots - 1)]
    row_tok = jnp.where(row_valid, row_slot // TOP_K, 0)
    row_w = jnp.where(row_valid, comb.reshape(-1)[row_slot], 0.0)
    row_dst = jnp.where(row_valid, row_slot, n_slots + row % (2 * MOE_TM))
    shape3 = (nb, 1, MOE_TM)
    return _moe_blocks(h, row_tok.reshape(shape3), row_dst.reshape(shape3), row_w.reshape(nb, MOE_TM, 1), block_e,
                       n_used, w1, b1, w2, b2, n_slots + 2 * MOE_TM)


def _resid_kernel(with_next, x_ref, u_ref, gf_ref, ng3_ref, *refs):
    d = x_ref.shape[1]
    y = u_ref[:, 0:d]
    for k in range(1, TOP_K):
        y = y + u_ref[:, k * d:(k + 1) * d]
    xn = x_ref[...] + gf_ref[0] * _rms(y, ng3_ref[...])
    if with_next:
        ng0_ref, sc_ref, sh_ref, xo_ref, ho_ref = refs
        ho_ref[...] = (_rms(xn, ng0_ref[...]) * (1.0 + sc_ref[0]) + sh_ref[0]).astype(ho_ref.dtype)
    else:
        (xo_ref,) = refs
    xo_ref[...] = xn


def _resid(xc, u, gf, ng3, nxt, seq, n_batch):
    n, d = xc.shape
    tm = _pow2_tile(256, seq, n - n_batch * seq)
    seg = _seg_map(seq // tm, n_batch)
    row = lambda i: (i, 0)
    vec = lambda i: (0, 0)
    in_specs = [pl.BlockSpec((tm, d), row), pl.BlockSpec((tm, TOP_K * d), row), pl.BlockSpec((1, 1, d), seg),
                pl.BlockSpec((1, d), vec)]
    args = [xc, u, gf, ng3.reshape(1, d)]
    out_specs = [pl.BlockSpec((tm, d), row)]
    out_shape = [jax.ShapeDtypeStruct((n, d), F32)]
    if nxt is not None:
        ng0, sc, sh = nxt
        in_specs += [pl.BlockSpec((1, d), vec), pl.BlockSpec((1, 1, d), seg), pl.BlockSpec((1, 1, d), seg)]
        args += [ng0.reshape(1, d), sc, sh]
        out_specs.append(pl.BlockSpec((tm, d), row))
        out_shape.append(jax.ShapeDtypeStruct((n, d), BF16))
    return pl.pallas_call(
        functools.partial(_resid_kernel, nxt is not None),
        grid=(n // tm,),
        in_specs=in_specs, out_specs=out_specs, out_shape=out_shape,
        compiler_params=_cparams(("parallel",)),
        name="resid",
    )(*args)


def _gqa_perm():
    idx = np.arange(KV_HEADS * Q_GROUP * HEAD_DIM).reshape(KV_HEADS, Q_GROUP, HEAD_DIM)
    return idx.transpose(1, 0, 2).reshape(-1)


def _w_in_perm():
    a_qk = A_HEADS * 2 * HEAD_DIM
    sizes = dict(aq=a_qk, ak=a_qk, av=A_HEADS * 2 * HEAD_DIM, bq=512, bk=128, bv=128, cq=512, ck=128, cv=128)
    off, o = {}, 0
    for k in ("aq", "ak", "av", "bq", "bk", "bv", "cq", "ck", "cv"):
        off[k] = o
        o += sizes[k]
    seg = lambda k: off[k] + np.arange(sizes[k])
    g = _gqa_perm()
    return np.concatenate([seg("aq"), seg("ak"), off["bq"] + g, off["cq"] + g,
                           seg("bk"), seg("ck"), seg("bv"), seg("cv"), seg("av")]), o


def _rope_tables(seq, n_ctx_rows):
    rows = seq // GRID_W
    row = jnp.repeat(jnp.arange(rows, dtype=F32), GRID_W)
    col = jnp.tile(jnp.arange(GRID_W, dtype=F32), rows)
    inv_freq = ROPE_THETA ** (-jnp.arange(AXIS_PAIRS, dtype=F32) / AXIS_PAIRS)
    ang_r = row[:, None] * inv_freq
    ang_c = col[:, None] * inv_freq
    ang = jnp.concatenate([ang_r, ang_r, ang_c, ang_c], axis=-1)
    cos, sin = jnp.cos(ang), jnp.sin(ang)
    first_half = (np.arange(HEAD_DIM) % (2 * AXIS_PAIRS)) < AXIS_PAIRS
    sa = jnp.where(first_half, -sin, 0.0)
    sb = jnp.where(first_half, 0.0, sin)
    return [jnp.tile(t, (1, LANES // HEAD_DIM)) for t in (cos, sa, sb)]


def kernel(x, c, ctx, c_ctx, ada_w, ada_b, norm_g, w_in, diff_lambda, diff_subln, qk_norm, sink, w_branch, w_out,
           router_w, router_b, w1, b1, w2, b2):
    n_batch, seq, d = x.shape
    ctx_len = ctx.shape[1]
    depth = ada_w.shape[0]
    n_lat, n_ctx = n_batch * seq, n_batch * ctx_len

    mod_rows = -(-(n_batch + 1) // 8) * 8
    cc = jnp.zeros((mod_rows, d), F32).at[:n_batch].set(c).at[n_batch].set(c_ctx)
    mod = _adaln(cc, ada_w, ada_b).reshape(depth, mod_rows, 6, 1, d).transpose(0, 2, 1, 3, 4)

    perm, n_qkv = _w_in_perm()
    perm = np.concatenate([perm, n_qkv + np.arange(w_in.shape[2] - n_qkv)])
    gperm = _gqa_perm()
    cos_l, sa_l, sb_l = _rope_tables(seq, n_ctx)
    ones = jnp.ones((n_ctx, LANES), F32)
    zeros = jnp.zeros((n_ctx, LANES), F32)
    cos_t = jnp.concatenate([jnp.tile(cos_l, (n_batch, 1)), ones])
    sa_t = jnp.concatenate([jnp.tile(sa_l, (n_batch, 1)), zeros])
    sb_t = jnp.concatenate([jnp.tile(sb_l, (n_batch, 1)), zeros])
    ne_pad = -(-N_EXPERTS // LANES) * LANES

    xc = jnp.concatenate([x.reshape(n_lat, d), ctx.reshape(n_ctx, d)])
    sh_m, sc_m = mod[0, 0], mod[0, 1]
    h = _norm_mod(xc, norm_g[0, 0], sc_m, sh_m, seq, n_batch)

    for l in range(depth):
        last = l == depth - 1
        lam_init = 0.8 - 0.6 * math.exp(-0.3 * l)
        sh_m, sc_m, g_m, sh_f, sc_f, g_f = (mod[l, k] for k in range(6))
        w_l = w_in[l][:, perm].astype(BF16)
        gq = jnp.tile(qk_norm[l, 0], LANES // HEAD_DIM).reshape(1, LANES)
        gk = jnp.tile(qk_norm[l, 1], LANES // HEAD_DIM).reshape(1, LANES)
        p = _proj(h, w_l, cos_t, sa_t, sb_t, gq, gk)

        ya, yb, yc = _mixers(p, seq, ctx_len, n_batch, lam_init, diff_lambda[l], diff_subln[l], sink[l], False)
        if not last:
            ya_c, yb_c, yc_c = _mixers(p, seq, ctx_len, n_batch, lam_init, diff_lambda[l], diff_subln[l], sink[l],
                                       True)
            ya, yb, yc = (jnp.concatenate(t) for t in ((ya, ya_c), (yb, yb_c), (yc, yc_c)))
        n_rows = ya.shape[0]

        wb = jnp.stack([w_branch[l, 0], w_branch[l, 1][gperm], w_branch[l, 2][gperm]]).astype(BF16)
        rw = jnp.zeros((d, ne_pad), F32).at[:, :N_EXPERTS].set(router_w[l])
        rb = jnp.zeros((1, ne_pad), F32).at[0, :N_EXPERTS].set(router_b[l])
        xc, h_f, logits = _merge(ya, yb, yc, p, wb, w_out[l].astype(BF16), xc, g_m, norm_g[l, 1],
                                 norm_g[l, 2], sc_f, sh_f, rw, rb, seq, n_batch)

        u = _moe(h_f, logits[:, :N_EXPERTS], w1[l], b1[l], w2[l], b2[l])
        if last:
            (xc,) = _resid(xc, u, g_f, norm_g[l, 3], None, seq, n_batch)
        else:
            xc, h = _resid(xc, u, g_f, norm_g[l, 3], (norm_g[l + 1, 0], mod[l + 1, 1], mod[l + 1, 0]), seq,
                           n_batch)
    return xc[:n_lat].reshape(n_batch, seq, d)
```

```python
import functools
import math

import jax
import jax.numpy as jnp
import numpy as np
from jax import lax
from jax.experimental import pallas as pl
from jax.experimental.pallas import tpu as pltpu

F32 = jnp.float32
BF16 = jnp.bfloat16

HEAD_DIM = 64
LANES = 128
GRID_W = 64
AXIS_PAIRS = HEAD_DIM // 4
ROPE_THETA = 10000.0
EPS = 1e-6
NEG_INF = -1e30
WINDOW = 128
A_HEADS = 4
KV_HEADS = 2
Q_GROUP = 4
N_EXPERTS = 32
TOP_K = 4
TOP_K_SHIFT = 2
SWIGLU_LIMIT = 7.0
SWIGLU_ALPHA = 1.702
LOG2E = math.log2(math.e)
Q_SCALE = HEAD_DIM ** -0.5 * LOG2E
PROJ_TN = 512
MOE_TM = 256
VMEM_LIMIT = 56 * 1024 * 1024

T_AQ, T_AK, T_BQ, T_CQ, T_KV, T_AV, T_GATE = 0, 1, 2, 3, 4, 5, 6


def _cparams(sem):
    return pltpu.CompilerParams(dimension_semantics=sem, vmem_limit_bytes=VMEM_LIMIT)


def _rms(x, g):
    return x * lax.rsqrt(jnp.mean(x * x, axis=-1, keepdims=True) + EPS) * g


def _sigmoid(x):
    return 1.0 / (1.0 + jnp.exp(-x))


def _pow2_tile(cap, *sizes):
    t = cap
    while any(s % t for s in sizes):
        t //= 2
    return t


def _adaln_kernel(c_ref, w_ref, b_ref, o_ref):
    c = c_ref[...]
    a = c * _sigmoid(c)
    o_ref[0] = jnp.dot(a, w_ref[0], preferred_element_type=F32,
                       precision=lax.Precision.HIGHEST) + b_ref[0]


def _adaln(cc, ada_w, ada_b):
    depth, d, n = ada_w.shape
    rows = cc.shape[0]
    tn = _pow2_tile(1024, n // 6)
    return pl.pallas_call(
        _adaln_kernel,
        grid=(depth, n // tn),
        in_specs=[pl.BlockSpec((rows, d), lambda l, j: (0, 0)),
                  pl.BlockSpec((1, d, tn), lambda l, j: (l, 0, j)),
                  pl.BlockSpec((1, 1, tn), lambda l, j: (l, 0, j))],
        out_specs=pl.BlockSpec((1, rows, tn), lambda l, j: (l, 0, j)),
        out_shape=jax.ShapeDtypeStruct((depth, rows, n), F32),
        compiler_params=_cparams(("parallel", "parallel")),
        name="adaln",
    )(cc, ada_w, ada_b.reshape(depth, 1, n))


def _norm_mod_kernel(x_ref, g_ref, sc_ref, sh_ref, h_ref):
    h = _rms(x_ref[...], g_ref[...]) * (1.0 + sc_ref[0]) + sh_ref[0]
    h_ref[...] = h.astype(h_ref.dtype)


def _seg_map(tiles_per_batch, n_batch):
    return lambda i: (jnp.minimum(i // tiles_per_batch, n_batch), 0, 0)


def _norm_mod(xc, g, sc, sh, seq, n_batch):
    n, d = xc.shape
    tm = _pow2_tile(1024, seq, n - n_batch * seq)
    seg = _seg_map(seq // tm, n_batch)
    return pl.pallas_call(
        _norm_mod_kernel,
        grid=(n // tm,),
        in_specs=[pl.BlockSpec((tm, d), lambda i: (i, 0)),
                  pl.BlockSpec((1, d), lambda i: (0, 0)),
                  pl.BlockSpec((1, 1, d), seg),
                  pl.BlockSpec((1, 1, d), seg)],
        out_specs=pl.BlockSpec((tm, d), lambda i: (i, 0)),
        out_shape=jax.ShapeDtypeStruct((n, d), BF16),
        compiler_params=_cparams(("parallel",)),
        name="norm_mod",
    )(xc, g.reshape(1, d), sc, sh)


_ROPE_S, _ROPE, _NORMQ, _NORMK, _PLAIN = range(5)
_TILE_OPS = {T_AQ: [_ROPE_S] * 4, T_AK: [_ROPE] * 4, T_BQ: [_NORMQ] * 4, T_CQ: [_ROPE_S] * 4,
             T_KV: [_NORMK, _ROPE, _PLAIN, _PLAIN]}


def _proj_kernel(h_ref, w_ref, cos_ref, sa_ref, sb_ref, gq_ref, gk_ref, o_ref, acc_ref):
    j = pl.program_id(1)
    acc_ref[...] = jnp.dot(h_ref[...], w_ref[...], preferred_element_type=F32)

    def rope(x):
        return (x * cos_ref[...] + pltpu.roll(x, LANES - 16, 1) * sa_ref[...]
                + pltpu.roll(x, 16, 1) * sb_ref[...])

    def head_norm(x, g):
        lo = lax.broadcasted_iota(jnp.int32, x.shape, 1) < HEAD_DIM
        x2 = x * x
        s_lo = jnp.sum(jnp.where(lo, x2, 0.0), axis=1, keepdims=True)
        s_hi = jnp.sum(jnp.where(lo, 0.0, x2), axis=1, keepdims=True)
        ms = jnp.where(lo, s_lo, s_hi) * (1.0 / HEAD_DIM)
        return x * lax.rsqrt(ms + EPS) * g

    def chunk(op, x):
        if op == _ROPE_S:
            return rope(x) * Q_SCALE
        if op == _ROPE:
            return rope(x)
        if op == _NORMQ:
            return rope(head_norm(x, gq_ref[...])) * Q_SCALE
        if op == _NORMK:
            return rope(head_norm(x, gk_ref[...]))
        return x

    for t, ops in _TILE_OPS.items():
        @pl.when(j == t)
        def _(ops=ops):
            for c, op in enumerate(ops):
                cs = slice(c * LANES, (c + 1) * LANES)
                o_ref[:, cs] = chunk(op, acc_ref[:, cs]).astype(o_ref.dtype)

    @pl.when(j >= T_AV)
    def _():
        o_ref[...] = acc_ref[...].astype(o_ref.dtype)


def _proj(h, w, cos, sa, sb, gq, gk):
    n, d = h.shape
    cols = w.shape[1]
    tm = _pow2_tile(1024, n)
    row = lambda i, j: (i, 0)
    vec = lambda i, j: (0, 0)
    return pl.pallas_call(
        _proj_kernel,
        grid=(n // tm, cols // PROJ_TN),
        in_specs=[pl.BlockSpec((tm, d), row),
                  pl.BlockSpec((d, PROJ_TN), lambda i, j: (0, j)),
                  pl.BlockSpec((tm, LANES), row),
                  pl.BlockSpec((tm, LANES), row),
                  pl.BlockSpec((tm, LANES), row),
                  pl.BlockSpec((1, LANES), vec),
                  pl.BlockSpec((1, LANES), vec)],
        out_specs=pl.BlockSpec((tm, PROJ_TN), lambda i, j: (i, j)),
        out_shape=jax.ShapeDtypeStruct((n, cols), BF16),
        scratch_shapes=[pltpu.VMEM((tm, PROJ_TN), F32)],
        compiler_params=_cparams(("parallel", "arbitrary")),
        name="proj",
    )(h, w, cos, sa, sb, gq, gk)


def _lane_lo(shape):
    return lax.broadcasted_iota(jnp.int32, shape, 1) < HEAD_DIM


def _split_heads(qv):
    lo = _lane_lo(qv.shape)
    zero = jnp.zeros_like(qv)
    return jnp.concatenate([jnp.where(lo, qv, zero), jnp.where(lo, zero, qv)], axis=0)


def _qk(lhs, k):
    return lax.dot_general(lhs, k, (((1,), (1,)), ((), ())), preferred_element_type=F32)


def _online_step(carry, lhs, k, v):
    m, l, acc = carry
    s = _qk(lhs, k)
    m_new = jnp.maximum(m, jnp.max(s, axis=1, keepdims=True))
    alpha = jnp.exp2(m - m_new)
    p = jnp.exp2((s - m_new).astype(BF16))
    l = alpha * l + jnp.sum(p.astype(F32), axis=1, keepdims=True)
    acc = alpha * acc + jnp.dot(p, v, preferred_element_type=F32)
    return m_new, l, acc


def _flash(lhs, segs, tk):
    rows = lhs.shape[0]
    carry = (jnp.full((rows, 1), NEG_INF, F32), jnp.zeros((rows, 1), F32), jnp.zeros((rows, LANES), F32))
    for k_ref, v_ref in segs:
        nk = k_ref.shape[0]
        t = min(tk, nk)
        for i in range(nk // t):
            carry = _online_step(carry, lhs, k_ref[i * t:(i + 1) * t, :], v_ref[i * t:(i + 1) * t, :])
    _, l, acc = carry
    return acc / l


def _attn_a_kernel(lam_init, n_seg, tk, q_ref, dl_ref, sg_ref, *refs):
    segs = [(refs[2 * i], refs[2 * i + 1]) for i in range(n_seg)]
    o_ref = refs[2 * n_seg]
    tq = q_ref.shape[0]
    o = _flash(_split_heads(q_ref[...]), segs, tk)
    dl = dl_ref[...]
    lam = (jnp.exp(jnp.sum(dl[0:1] * dl[1:2], axis=1, keepdims=True))
           - jnp.exp(jnp.sum(dl[2:3] * dl[3:4], axis=1, keepdims=True)) + lam_init)
    y = o[:tq] - lam * o[tq:]
    o_ref[...] = (_rms(y, sg_ref[...]) * (1.0 - lam_init)).astype(o_ref.dtype)


def _gqa_lhs(q_ref):
    return jnp.concatenate([_split_heads(q_ref[:, j * LANES:(j + 1) * LANES]) for j in range(Q_GROUP)], axis=0)


def _gqa_store(o, o_ref):
    tq = o_ref.shape[0]
    lo = _lane_lo((tq, LANES))
    for j in range(Q_GROUP):
        a = o[(2 * j) * tq:(2 * j + 1) * tq]
        b = o[(2 * j + 1) * tq:(2 * j + 2) * tq]
        o_ref[:, j * LANES:(j + 1) * LANES] = jnp.where(lo, a, b).astype(o_ref.dtype)


def _attn_b_kernel(n_seg, tk, q_ref, *refs):
    segs = [(refs[2 * i], refs[2 * i + 1]) for i in range(n_seg)]
    o_ref = refs[2 * n_seg]
    _gqa_store(_flash(_gqa_lhs(q_ref), segs, tk), o_ref)


def _attn_c_kernel(span, has_lat, sink_ref, q_ref, *refs):
    if has_lat:
        kl_ref, vl_ref, kc_ref, vc_ref, o_ref = refs
    else:
        kc_ref, vc_ref, o_ref = refs
    tq = q_ref.shape[0]
    lhs = _gqa_lhs(q_ref)
    rows = lhs.shape[0]
    sink = jnp.concatenate(
        [jnp.full((tq, 1), sink_ref[n * Q_GROUP + j] * LOG2E, F32) for j in range(Q_GROUP) for n in range(KV_HEADS)],
        axis=0)
    kc = kc_ref[...]
    vc = vc_ref[...]
    s_c = _qk(lhs, kc)
    m = jnp.maximum(jnp.max(s_c, axis=1, keepdims=True), sink)
    if has_lat:
        n_lat = kl_ref.shape[0]
        q0 = pl.program_id(1) * tq
        start = pl.multiple_of(jnp.clip(q0 - WINDOW, 0, n_lat - span), WINDOW)
        kw = kl_ref[pl.ds(start, span), :]
        vw = vl_ref[pl.ds(start, span), :]
        s_w = _qk(lhs, kw)
        qi = lax.broadcasted_iota(jnp.int32, (rows, span), 0) & (tq - 1)
        ki = lax.broadcasted_iota(jnp.int32, (rows, span), 1)
        dist = qi - ki + (q0 - start)
        s_w = jnp.where(jnp.abs(dist) <= WINDOW, s_w, NEG_INF)
        m = jnp.maximum(m, jnp.max(s_w, axis=1, keepdims=True))
    p_c = jnp.exp2((s_c - m).astype(BF16))
    l = jnp.sum(p_c.astype(F32), axis=1, keepdims=True) + jnp.exp2(sink - m)
    acc = jnp.dot(p_c, vc, preferred_element_type=F32)
    if has_lat:
        p_w = jnp.exp2((s_w - m).astype(BF16))
        l = l + jnp.sum(p_w.astype(F32), axis=1, keepdims=True)
        acc = acc + jnp.dot(p_w, vw, preferred_element_type=F32)
    _gqa_store(acc / l, o_ref)


def _mixers(p, seq, ctx_len, n_batch, lam_init, diff_lambda, subln, sink_perm, ctx_queries):
    n_lat = n_batch * seq
    cb = PROJ_TN // LANES
    if ctx_queries:
        tq, nq = ctx_len, 1
        tqa, nqa = tq, nq
        q_row = lambda b, i: n_lat // ctx_len + b
        qa_row = q_row
    else:
        tq = _pow2_tile(128, seq)
        nq = seq // tq
        q_row = lambda b, i: b * nq + i
        tqa = _pow2_tile(512, seq)
        nqa = seq // tqa
        qa_row = lambda b, i: b * nqa + i
    out_rows = n_batch * nq * tq
    tk = 512

    def kv_specs(col, grid_rank):
        def wrap(f):
            return (lambda b, h, i: f(b, h)) if grid_rank == 3 else (lambda b, i: f(b, 0))
        specs = []
        if not ctx_queries:
            specs.append(pl.BlockSpec((seq, LANES), wrap(lambda b, h: (b, col(h)))))
        specs.append(pl.BlockSpec((ctx_len, LANES), wrap(lambda b, h: (n_lat // ctx_len + b, col(h)))))
        return specs

    n_seg = 1 if ctx_queries else 2

    ka = kv_specs(lambda h: T_AK * cb + h, 3)
    va = kv_specs(lambda h: T_AV * cb + h, 3)
    ya = pl.pallas_call(
        functools.partial(_attn_a_kernel, lam_init, n_seg, tk),
        grid=(n_batch, A_HEADS, nqa),
        in_specs=[pl.BlockSpec((tqa, LANES), lambda b, h, i: (qa_row(b, i), T_AQ * cb + h)),
                  pl.BlockSpec(diff_lambda.shape, lambda b, h, i: (0, 0)),
                  pl.BlockSpec((1, LANES), lambda b, h, i: (0, 0))]
        + [s for kv in zip(ka, va) for s in kv],
        out_specs=pl.BlockSpec((tqa, LANES), lambda b, h, i: (b * nqa + i, h)),
        out_shape=jax.ShapeDtypeStruct((out_rows, A_HEADS * LANES), BF16),
        compiler_params=_cparams(("parallel", "parallel", "arbitrary")),
        name="attn_a_ctx" if ctx_queries else "attn_a",
    )(p, diff_lambda, subln.reshape(1, LANES), *([p] * (2 * n_seg)))

    kb = kv_specs(lambda h: T_KV * cb + 0, 2)
    vb = kv_specs(lambda h: T_KV * cb + 2, 2)
    yb = pl.pallas_call(
        functools.partial(_attn_b_kernel, n_seg, tk),
        grid=(n_batch, nq),
        in_specs=[pl.BlockSpec((tq, PROJ_TN), lambda b, i: (q_row(b, i), T_BQ))]
        + [s for kv in zip(kb, vb) for s in kv],
        out_specs=pl.BlockSpec((tq, PROJ_TN), lambda b, i: (b * nq + i, 0)),
        out_shape=jax.ShapeDtypeStruct((out_rows, PROJ_TN), BF16),
        compiler_params=_cparams(("parallel", "arbitrary")),
        name="attn_b_ctx" if ctx_queries else "attn_b",
    )(p, *([p] * (2 * n_seg)))

    kc = kv_specs(lambda h: T_KV * cb + 1, 2)
    vc = kv_specs(lambda h: T_KV * cb + 3, 2)
    span = min(tq + 2 * WINDOW, seq)
    yc = pl.pallas_call(
        functools.partial(_attn_c_kernel, span, not ctx_queries),
        grid=(n_batch, nq),
        in_specs=[pl.BlockSpec(memory_space=pltpu.SMEM),
                  pl.BlockSpec((tq, PROJ_TN), lambda b, i: (q_row(b, i), T_CQ))]
        + [s for kv in zip(kc, vc) for s in kv],
        out_specs=pl.BlockSpec((tq, PROJ_TN), lambda b, i: (b * nq + i, 0)),
        out_shape=jax.ShapeDtypeStruct((out_rows, PROJ_TN), BF16),
        compiler_params=_cparams(("parallel", "arbitrary")),
        name="attn_c_ctx" if ctx_queries else "attn_c",
    )(sink_perm, p, *([p] * (2 * n_seg)))
    return ya, yb, yc


def _merge_kernel(ya_ref, yb_ref, yc_ref, g0_ref, g1_ref, g2_ref, wb_ref, wo_ref, x_ref, gm_ref, ng1_ref,
                  ng2_ref, scf_ref, shf_ref, rw_ref, rb_ref, xo_ref, ho_ref, ti_ref, tw_ref):
    mix = None
    for y_ref, g_ref, i in ((ya_ref, g0_ref, 0), (yb_ref, g1_ref, 1), (yc_ref, g2_ref, 2)):
        t = _sigmoid(g_ref[...].astype(F32)) * jnp.dot(y_ref[...], wb_ref[i], preferred_element_type=F32)
        mix = t if mix is None else mix + t
    y = jnp.dot(mix.astype(BF16), wo_ref[...], preferred_element_type=F32)
    xn = x_ref[...] + gm_ref[0] * _rms(y, ng1_ref[...])
    xo_ref[...] = xn
    h = _rms(xn, ng2_ref[...]) * (1.0 + scf_ref[0]) + shf_ref[0]
    ho_ref[...] = h.astype(ho_ref.dtype)
    logits = jnp.dot(h, rw_ref[...], preferred_element_type=F32,
                     precision=lax.Precision.HIGHEST) + rb_ref[...]
    lane = lax.broadcasted_iota(jnp.int32, logits.shape, 1)
    lane_f = lane.astype(F32)
    lg = jnp.where(lane < N_EXPERTS, logits, -jnp.inf)
    sel_i = jnp.zeros(logits.shape, F32)
    sel_w = jnp.zeros(logits.shape, F32)
    denom = None
    top = None
    for k in range(TOP_K):
        m = jnp.max(lg, axis=1, keepdims=True)
        idx = jnp.min(jnp.where(lg == m, lane_f, float(LANES)), axis=1, keepdims=True)
        top = m if top is None else top
        e = jnp.exp(m - top)
        denom = e if denom is None else denom + e
        sel_i = jnp.where(lane == k, idx, sel_i)
        sel_w = jnp.where(lane == k, e, sel_w)
        lg = jnp.where(lane_f == idx, -jnp.inf, lg)
    ti_ref[...] = sel_i.astype(jnp.int32)
    tw_ref[...] = sel_w / denom


def _merge(ya, yb, yc, p, wb, wo, xc, gm, ng1, ng2, scf, shf, rw, rb, seq, n_batch):
    n = ya.shape[0]
    d = xc.shape[1]
    tm = _pow2_tile(512, seq, n - n_batch * seq)
    seg = _seg_map(seq // tm, n_batch)
    row = lambda i: (i, 0)
    vec = lambda i: (0, 0)
    gate0 = T_GATE * PROJ_TN // d
    ne = rw.shape[1]
    return pl.pallas_call(
        _merge_kernel,
        grid=(n // tm,),
        in_specs=[pl.BlockSpec((tm, PROJ_TN), row)] * 3
        + [pl.BlockSpec((tm, d), lambda i, k=k: (i, gate0 + k)) for k in range(3)]
        + [pl.BlockSpec(wb.shape, lambda i: (0, 0, 0)),
           pl.BlockSpec(wo.shape, vec),
           pl.BlockSpec((tm, d), row),
           pl.BlockSpec((1, 1, d), seg),
           pl.BlockSpec((1, d), vec),
           pl.BlockSpec((1, d), vec),
           pl.BlockSpec((1, 1, d), seg),
           pl.BlockSpec((1, 1, d), seg),
           pl.BlockSpec(rw.shape, vec),
           pl.BlockSpec((1, ne), vec)],
        out_specs=[pl.BlockSpec((tm, d), row), pl.BlockSpec((tm, d), row), pl.BlockSpec((tm, ne), row),
                   pl.BlockSpec((tm, ne), row)],
        out_shape=[jax.ShapeDtypeStruct((n, d), F32), jax.ShapeDtypeStruct((n, d), F32),
                   jax.ShapeDtypeStruct((n, ne), jnp.int32), jax.ShapeDtypeStruct((n, ne), F32)],
        compiler_params=_cparams(("parallel",)),
        name="merge",
    )(ya, yb, yc, p, p, p, wb, wo, xc, gm, ng1.reshape(1, d), ng2.reshape(1, d), scf, shf, rw, rb)


def _row_copies(n_rows, make):
    def start():
        for r in range(n_rows):
            make(r).start()

    def wait():
        for r in range(n_rows):
            make(r).wait()
    return start, wait


def _moe_kernel(be_ref, nu_ref, tok_ref, tok_next_ref, drow_ref, dcol_ref, h_hbm, w1_ref, b1_ref, w2_ref, b2_ref, cw_ref,
                u_hbm, xbuf, ybuf, w1b, w2b, gsem, ssem):
    i = pl.program_id(0)
    n_used = nu_ref[0]
    tm, d = xbuf.shape[1:]
    slot = i % 2

    def gather(idx_ref, s):
        return _row_copies(
            tm, lambda r: pltpu.make_async_copy(h_hbm.at[idx_ref[0, 0, r]], xbuf.at[s, r], gsem.at[s]))

    def scatter(s):
        def make(r):
            col = pl.multiple_of(dcol_ref[0, 0, r], d)
            return pltpu.make_async_copy(ybuf.at[s, r], u_hbm.at[drow_ref[0, 0, r], pl.ds(col, d)], ssem.at[s])
        return _row_copies(tm, make)

    @pl.when(i < n_used)
    def _():
        @pl.when(i == 0)
        def _():
            gather(tok_ref, slot)[0]()
            ybuf[1] = jnp.zeros(ybuf.shape[1:], ybuf.dtype)
            tail = 2 * tm // TOP_K
            for k in range(TOP_K):
                fill = pltpu.make_async_copy(ybuf.at[1, pl.ds(0, tail)],
                                             u_hbm.at[pl.ds(u_hbm.shape[0] - tail, tail), pl.ds(k * d, d)], ssem.at[1])
                fill.start()
                fill.wait()

        @pl.when(i + 1 < n_used)
        def _():
            gather(tok_next_ref, 1 - slot)[0]()

        gather(tok_ref, slot)[1]()

        @pl.when(i >= 2)
        def _():
            scatter(slot)[1]()

        @pl.when((i == 0) | (be_ref[i] != be_ref[jnp.maximum(i - 1, 0)]))
        def _():
            w1b[...] = w1_ref[0].astype(BF16)
            w2b[...] = w2_ref[0].astype(BF16)

        f = w2_ref.shape[1]
        x = xbuf[slot].astype(BF16)
        gu = jnp.dot(x, w1b[...], preferred_element_type=F32) + b1_ref[0]
        gate = jnp.minimum(gu[:, :f], SWIGLU_LIMIT)
        up = jnp.clip(gu[:, f:], -SWIGLU_LIMIT, SWIGLU_LIMIT)
        act = (up + 1.0) * (gate * _sigmoid(SWIGLU_ALPHA * gate))
        y = jnp.dot(act.astype(BF16), w2b[...], preferred_element_type=F32) + b2_ref[0]
        ybuf[slot] = y * cw_ref[0]
        scatter(slot)[0]()

        @pl.when(i == n_used - 1)
        def _():
            @pl.when(i >= 1)
            def _():
                scatter(1 - slot)[1]()
            scatter(slot)[1]()


def _moe_blocks(h, row_tok, row_dst, row_w, block_e, n_used, layer, w1, b1, w2, b2, n_out):
    d = h.shape[1]
    depth, ne, _, f2 = w1.shape
    f = w2.shape[2]
    nb = row_tok.shape[0]
    blk = lambda i, be, nu: (jnp.minimum(i, nu[0] - 1), 0, 0)
    blk_next = lambda i, be, nu: (jnp.minimum(i + 1, nu[0] - 1), 0, 0)
    wexp = lambda i, be, nu: (layer, be[i], 0, 0)
    smem_blk = lambda m: pl.BlockSpec((1, 1, MOE_TM), m, memory_space=pltpu.SMEM)
    return pl.pallas_call(
        _moe_kernel,
        grid_spec=pltpu.PrefetchScalarGridSpec(
            num_scalar_prefetch=2,
            grid=(nb,),
            in_specs=[smem_blk(blk), smem_blk(blk_next), smem_blk(blk), smem_blk(blk),
                      pl.BlockSpec(memory_space=pl.ANY),
                      pl.BlockSpec((None, 1, d, f2), wexp),
                      pl.BlockSpec((None, 1, 1, f2), wexp),
                      pl.BlockSpec((None, 1, f, d), wexp),
                      pl.BlockSpec((None, 1, 1, d), wexp),
                      pl.BlockSpec((1, MOE_TM, 1), blk)],
            out_specs=pl.BlockSpec(memory_space=pl.ANY),
            scratch_shapes=[pltpu.VMEM((2, MOE_TM, d), F32), pltpu.VMEM((2, MOE_TM, d), F32),
                            pltpu.VMEM((d, f2), BF16), pltpu.VMEM((f, d), BF16),
                            pltpu.SemaphoreType.DMA((2,)), pltpu.SemaphoreType.DMA((2,))],
        ),
        out_shape=jax.ShapeDtypeStruct((n_out // TOP_K, TOP_K * d), F32),
        compiler_params=_cparams(("arbitrary",)),
        name="moe",
    )(block_e, n_used, row_tok, row_tok, row_dst >> TOP_K_SHIFT, (row_dst & (TOP_K - 1)) * d, h,
      w1, b1.reshape(depth, ne, 1, f2), w2, b2.reshape(depth, ne, 1, d), row_w)


def _moe(h, top_idx, comb, layer, w1, b1, w2, b2):
    n, d = h.shape
    n_slots = n * TOP_K
    flat_e = top_idx.reshape(-1).astype(jnp.int32)
    order = jnp.argsort(flat_e).astype(jnp.int32)
    counts = jnp.sum(flat_e[:, None] == jnp.arange(N_EXPERTS, dtype=jnp.int32), axis=0, dtype=jnp.int32)
    padded = (counts + MOE_TM - 1) // MOE_TM * MOE_TM
    start = jnp.cumsum(counts) - counts
    pad_end = jnp.cumsum(padded)
    pad_start = pad_end - padded
    nb = -(-n_slots // MOE_TM) + N_EXPERTS
    rows = nb * MOE_TM
    block_e = jnp.minimum(jnp.searchsorted(pad_end, jnp.arange(nb, dtype=jnp.int32) * MOE_TM, side='right'),
                          N_EXPERTS - 1).astype(jnp.int32)
    n_used = (pad_end[-1:] // MOE_TM).astype(jnp.int32)
    row_e = jnp.repeat(block_e, MOE_TM)
    row = jnp.arange(rows, dtype=jnp.int32)
    row_j = row - pad_start[row_e]
    row_valid = row_j < counts[row_e]
    row_slot = order[jnp.clip(start[row_e] + row_j, 0, n_slots - 1)]
    row_tok = jnp.where(row_valid, row_slot // TOP_K, 0)
    row_w = jnp.where(row_valid, comb.reshape(-1)[row_slot], 0.0)
    row_dst = jnp.where(row_valid, row_slot, n_slots + row % (2 * MOE_TM))
    shape3 = (nb, 1, MOE_TM)
    return _moe_blocks(h, row_tok.reshape(shape3), row_dst.reshape(shape3), row_w.reshape(nb, MOE_TM, 1), block_e,
                       n_used, layer, w1, b1, w2, b2, n_slots + 2 * MOE_TM)


def _resid_kernel(with_next, x_ref, u_ref, gf_ref, ng3_ref, *refs):
    d = x_ref.shape[1]
    y = u_ref[:, 0:d]
    for k in range(1, TOP_K):
        y = y + u_ref[:, k * d:(k + 1) * d]
    xn = x_ref[...] + gf_ref[0] * _rms(y, ng3_ref[...])
    if with_next:
        ng0_ref, sc_ref, sh_ref, xo_ref, ho_ref = refs
        ho_ref[...] = (_rms(xn, ng0_ref[...]) * (1.0 + sc_ref[0]) + sh_ref[0]).astype(ho_ref.dtype)
    else:
        (xo_ref,) = refs
    xo_ref[...] = xn


def _resid(xc, u, gf, ng3, nxt, seq, n_batch):
    n, d = xc.shape
    tm = _pow2_tile(256, seq, n - n_batch * seq)
    seg = _seg_map(seq // tm, n_batch)
    row = lambda i: (i, 0)
    vec = lambda i: (0, 0)
    in_specs = [pl.BlockSpec((tm, d), row), pl.BlockSpec((tm, TOP_K * d), row), pl.BlockSpec((1, 1, d), seg),
                pl.BlockSpec((1, d), vec)]
    args = [xc, u, gf, ng3.reshape(1, d)]
    out_specs = [pl.BlockSpec((tm, d), row)]
    out_shape = [jax.ShapeDtypeStruct((n, d), F32)]
    if nxt is not None:
        ng0, sc, sh = nxt
        in_specs += [pl.BlockSpec((1, d), vec), pl.BlockSpec((1, 1, d), seg), pl.BlockSpec((1, 1, d), seg)]
        args += [ng0.reshape(1, d), sc, sh]
        out_specs.append(pl.BlockSpec((tm, d), row))
        out_shape.append(jax.ShapeDtypeStruct((n, d), BF16))
    return pl.pallas_call(
        functools.partial(_resid_kernel, nxt is not None),
        grid=(n // tm,),
        in_specs=in_specs, out_specs=out_specs, out_shape=out_shape,
        compiler_params=_cparams(("parallel",)),
        name="resid",
    )(*args)


def _gqa_perm():
    idx = np.arange(KV_HEADS * Q_GROUP * HEAD_DIM).reshape(KV_HEADS, Q_GROUP, HEAD_DIM)
    return idx.transpose(1, 0, 2).reshape(-1)


def _w_in_perm():
    a_qk = A_HEADS * 2 * HEAD_DIM
    sizes = dict(aq=a_qk, ak=a_qk, av=A_HEADS * 2 * HEAD_DIM, bq=512, bk=128, bv=128, cq=512, ck=128, cv=128)
    off, o = {}, 0
    for k in ("aq", "ak", "av", "bq", "bk", "bv", "cq", "ck", "cv"):
        off[k] = o
        o += sizes[k]
    seg = lambda k: off[k] + np.arange(sizes[k])
    g = _gqa_perm()
    return np.concatenate([seg("aq"), seg("ak"), off["bq"] + g, off["cq"] + g,
                           seg("bk"), seg("ck"), seg("bv"), seg("cv"), seg("av")]), o


def _rope_tables(seq, n_ctx_rows):
    rows = seq // GRID_W
    row = jnp.repeat(jnp.arange(rows, dtype=F32), GRID_W)
    col = jnp.tile(jnp.arange(GRID_W, dtype=F32), rows)
    inv_freq = ROPE_THETA ** (-jnp.arange(AXIS_PAIRS, dtype=F32) / AXIS_PAIRS)
    ang_r = row[:, None] * inv_freq
    ang_c = col[:, None] * inv_freq
    ang = jnp.concatenate([ang_r, ang_r, ang_c, ang_c], axis=-1)
    cos, sin = jnp.cos(ang), jnp.sin(ang)
    first_half = (np.arange(HEAD_DIM) % (2 * AXIS_PAIRS)) < AXIS_PAIRS
    sa = jnp.where(first_half, -sin, 0.0)
    sb = jnp.where(first_half, 0.0, sin)
    return [jnp.tile(t, (1, LANES // HEAD_DIM)) for t in (cos, sa, sb)]


def kernel(x, c, ctx, c_ctx, ada_w, ada_b, norm_g, w_in, diff_lambda, diff_subln, qk_norm, sink, w_branch, w_out,
           router_w, router_b, w1, b1, w2, b2):
    n_batch, seq, d = x.shape
    ctx_len = ctx.shape[1]
    depth = ada_w.shape[0]
    n_lat, n_ctx = n_batch * seq, n_batch * ctx_len

    mod_rows = -(-(n_batch + 1) // 8) * 8
    cc = jnp.zeros((mod_rows, d), F32).at[:n_batch].set(c).at[n_batch].set(c_ctx)
    mod = _adaln(cc, ada_w, ada_b).reshape(depth, mod_rows, 6, 1, d).transpose(0, 2, 1, 3, 4)

    perm, n_qkv = _w_in_perm()
    perm = np.concatenate([perm, n_qkv + np.arange(w_in.shape[2] - n_qkv)])
    gperm = _gqa_perm()
    cos_l, sa_l, sb_l = _rope_tables(seq, n_ctx)
    ones = jnp.ones((n_ctx, LANES), F32)
    zeros = jnp.zeros((n_ctx, LANES), F32)
    cos_t = jnp.concatenate([jnp.tile(cos_l, (n_batch, 1)), ones])
    sa_t = jnp.concatenate([jnp.tile(sa_l, (n_batch, 1)), zeros])
    sb_t = jnp.concatenate([jnp.tile(sb_l, (n_batch, 1)), zeros])
    ne_pad = -(-N_EXPERTS // LANES) * LANES

    xc = jnp.concatenate([x.reshape(n_lat, d), ctx.reshape(n_ctx, d)])
    sh_m, sc_m = mod[0, 0], mod[0, 1]
    h = _norm_mod(xc, norm_g[0, 0], sc_m, sh_m, seq, n_batch)

    for l in range(depth):
        last = l == depth - 1
        lam_init = 0.8 - 0.6 * math.exp(-0.3 * l)
        sh_m, sc_m, g_m, sh_f, sc_f, g_f = (mod[l, k] for k in range(6))
        w_l = w_in[l][:, perm].astype(BF16)
        gq = jnp.tile(qk_norm[l, 0], LANES // HEAD_DIM).reshape(1, LANES)
        gk = jnp.tile(qk_norm[l, 1], LANES // HEAD_DIM).reshape(1, LANES)
        p = _proj(h, w_l, cos_t, sa_t, sb_t, gq, gk)

        ya, yb, yc = _mixers(p, seq, ctx_len, n_batch, lam_init, diff_lambda[l], diff_subln[l], sink[l], False)
        if not last:
            ya_c, yb_c, yc_c = _mixers(p, seq, ctx_len, n_batch, lam_init, diff_lambda[l], diff_subln[l], sink[l],
                                       True)
            ya, yb, yc = (jnp.concatenate(t) for t in ((ya, ya_c), (yb, yb_c), (yc, yc_c)))
        n_rows = ya.shape[0]

        wb = jnp.stack([w_branch[l, 0], w_branch[l, 1][gperm], w_branch[l, 2][gperm]]).astype(BF16)
        rw = jnp.zeros((d, ne_pad), F32).at[:, :N_EXPERTS].set(router_w[l])
        rb = jnp.zeros((1, ne_pad), F32).at[0, :N_EXPERTS].set(router_b[l])
        xc, h_f, top_i, top_w = _merge(ya, yb, yc, p, wb, w_out[l].astype(BF16), xc, g_m, norm_g[l, 1],
                                       norm_g[l, 2], sc_f, sh_f, rw, rb, seq, n_batch)

        u = _moe(h_f, top_i[:, :TOP_K], top_w[:, :TOP_K], l, w1, b1, w2, b2)
        if last:
            (xc,) = _resid(xc, u, g_f, norm_g[l, 3], None, seq, n_batch)
        else:
            xc, h = _resid(xc, u, g_f, norm_g[l, 3], (norm_g[l + 1, 0], mod[l + 1, 1], mod[l + 1, 0]), seq,
                           n_batch)
    return xc[:n_lat].reshape(n_batch, seq, d)
```

```python
import functools
import math

import jax
import jax.numpy as jnp
import numpy as np
from jax import lax
from jax.experimental import pallas as pl
from jax.experimental.pallas import tpu as pltpu

F32 = jnp.float32
BF16 = jnp.bfloat16

HEAD_DIM = 64
LANES = 128
GRID_W = 64
AXIS_PAIRS = HEAD_DIM // 4
ROPE_THETA = 10000.0
EPS = 1e-6
NEG_INF = -1e30
WINDOW = 128
A_HEADS = 4
KV_HEADS = 2
Q_GROUP = 4
N_EXPERTS = 32
TOP_K = 4
TOP_K_SHIFT = 2
SWIGLU_LIMIT = 7.0
SWIGLU_ALPHA = 1.702
LOG2E = math.log2(math.e)
Q_SCALE = HEAD_DIM ** -0.5 * LOG2E
PROJ_TN = 512
MOE_TM = 256
VMEM_LIMIT = 56 * 1024 * 1024

T_AQ, T_AK, T_BQ, T_CQ, T_KV, T_AV, T_GATE = 0, 1, 2, 3, 4, 5, 6


def _cparams(sem):
    return pltpu.CompilerParams(dimension_semantics=sem, vmem_limit_bytes=VMEM_LIMIT)


def _rms(x, g):
    return x * lax.rsqrt(jnp.mean(x * x, axis=-1, keepdims=True) + EPS) * g


def _sigmoid(x):
    return 1.0 / (1.0 + jnp.exp(-x))


def _pow2_tile(cap, *sizes):
    t = cap
    while any(s % t for s in sizes):
        t //= 2
    return t


def _adaln_kernel(c_ref, w_ref, b_ref, o_ref):
    c = c_ref[...]
    a = c * _sigmoid(c)
    o_ref[0] = jnp.dot(a, w_ref[0], preferred_element_type=F32,
                       precision=lax.Precision.HIGHEST) + b_ref[0]


def _adaln(cc, ada_w, ada_b):
    depth, d, n = ada_w.shape
    rows = cc.shape[0]
    tn = _pow2_tile(1024, n // 6)
    return pl.pallas_call(
        _adaln_kernel,
        grid=(depth, n // tn),
        in_specs=[pl.BlockSpec((rows, d), lambda l, j: (0, 0)),
                  pl.BlockSpec((1, d, tn), lambda l, j: (l, 0, j)),
                  pl.BlockSpec((1, 1, tn), lambda l, j: (l, 0, j))],
        out_specs=pl.BlockSpec((1, rows, tn), lambda l, j: (l, 0, j)),
        out_shape=jax.ShapeDtypeStruct((depth, rows, n), F32),
        compiler_params=_cparams(("parallel", "parallel")),
        name="adaln",
    )(cc, ada_w, ada_b.reshape(depth, 1, n))


def _norm_mod_kernel(x_ref, g_ref, sc_ref, sh_ref, h_ref):
    h = _rms(x_ref[...], g_ref[...]) * (1.0 + sc_ref[0]) + sh_ref[0]
    h_ref[...] = h.astype(h_ref.dtype)


def _seg_map(tiles_per_batch, n_batch):
    return lambda i: (jnp.minimum(i // tiles_per_batch, n_batch), 0, 0)


def _norm_mod(xc, g, sc, sh, seq, n_batch):
    n, d = xc.shape
    tm = _pow2_tile(1024, seq, n - n_batch * seq)
    seg = _seg_map(seq // tm, n_batch)
    return pl.pallas_call(
        _norm_mod_kernel,
        grid=(n // tm,),
        in_specs=[pl.BlockSpec((tm, d), lambda i: (i, 0)),
                  pl.BlockSpec((1, d), lambda i: (0, 0)),
                  pl.BlockSpec((1, 1, d), seg),
                  pl.BlockSpec((1, 1, d), seg)],
        out_specs=pl.BlockSpec((tm, d), lambda i: (i, 0)),
        out_shape=jax.ShapeDtypeStruct((n, d), BF16),
        compiler_params=_cparams(("parallel",)),
        name="norm_mod",
    )(xc, g.reshape(1, d), sc, sh)


_ROPE_S, _ROPE, _NORMQ, _NORMK, _PLAIN = range(5)
_TILE_OPS = {T_AQ: [_ROPE_S] * 4, T_AK: [_ROPE] * 4, T_BQ: [_NORMQ] * 4, T_CQ: [_ROPE_S] * 4,
             T_KV: [_NORMK, _ROPE, _PLAIN, _PLAIN]}


def _proj_kernel(h_ref, w_ref, cos_ref, sa_ref, sb_ref, gq_ref, gk_ref, o_ref, acc_ref):
    j = pl.program_id(1)
    acc_ref[...] = jnp.dot(h_ref[...], w_ref[...], preferred_element_type=F32)

    def rope(x):
        return (x * cos_ref[...] + pltpu.roll(x, LANES - 16, 1) * sa_ref[...]
                + pltpu.roll(x, 16, 1) * sb_ref[...])

    def head_norm(x, g):
        lo = lax.broadcasted_iota(jnp.int32, x.shape, 1) < HEAD_DIM
        x2 = x * x
        s_lo = jnp.sum(jnp.where(lo, x2, 0.0), axis=1, keepdims=True)
        s_hi = jnp.sum(jnp.where(lo, 0.0, x2), axis=1, keepdims=True)
        ms = jnp.where(lo, s_lo, s_hi) * (1.0 / HEAD_DIM)
        return x * lax.rsqrt(ms + EPS) * g

    def chunk(op, x):
        if op == _ROPE_S:
            return rope(x) * Q_SCALE
        if op == _ROPE:
            return rope(x)
        if op == _NORMQ:
            return rope(head_norm(x, gq_ref[...])) * Q_SCALE
        if op == _NORMK:
            return rope(head_norm(x, gk_ref[...]))
        return x

    for t, ops in _TILE_OPS.items():
        @pl.when(j == t)
        def _(ops=ops):
            for c, op in enumerate(ops):
                cs = slice(c * LANES, (c + 1) * LANES)
                o_ref[:, cs] = chunk(op, acc_ref[:, cs]).astype(o_ref.dtype)

    @pl.when(j >= T_AV)
    def _():
        o_ref[...] = acc_ref[...].astype(o_ref.dtype)


def _proj(h, w, cos, sa, sb, gq, gk):
    n, d = h.shape
    cols = w.shape[1]
    tm = _pow2_tile(1024, n)
    row = lambda i, j: (i, 0)
    vec = lambda i, j: (0, 0)
    return pl.pallas_call(
        _proj_kernel,
        grid=(n // tm, cols // PROJ_TN),
        in_specs=[pl.BlockSpec((tm, d), row),
                  pl.BlockSpec((d, PROJ_TN), lambda i, j: (0, j)),
                  pl.BlockSpec((tm, LANES), row),
                  pl.BlockSpec((tm, LANES), row),
                  pl.BlockSpec((tm, LANES), row),
                  pl.BlockSpec((1, LANES), vec),
                  pl.BlockSpec((1, LANES), vec)],
        out_specs=pl.BlockSpec((tm, PROJ_TN), lambda i, j: (i, j)),
        out_shape=jax.ShapeDtypeStruct((n, cols), BF16),
        scratch_shapes=[pltpu.VMEM((tm, PROJ_TN), F32)],
        compiler_params=_cparams(("parallel", "arbitrary")),
        name="proj",
    )(h, w, cos, sa, sb, gq, gk)


def _lane_lo(shape):
    return lax.broadcasted_iota(jnp.int32, shape, 1) < HEAD_DIM


def _split_heads(qv):
    lo = _lane_lo(qv.shape)
    zero = jnp.zeros_like(qv)
    return jnp.concatenate([jnp.where(lo, qv, zero), jnp.where(lo, zero, qv)], axis=0)


def _qk(lhs, k):
    return lax.dot_general(lhs, k, (((1,), (1,)), ((), ())), preferred_element_type=F32)


def _online_step(carry, lhs, k, v):
    m, l, acc = carry
    s = _qk(lhs, k)
    m_new = jnp.maximum(m, jnp.max(s, axis=1, keepdims=True))
    alpha = jnp.exp2(m - m_new)
    p = jnp.exp2((s - m_new).astype(BF16))
    l = alpha * l + jnp.sum(p.astype(F32), axis=1, keepdims=True)
    acc = alpha * acc + jnp.dot(p, v, preferred_element_type=F32)
    return m_new, l, acc


def _flash(lhs, segs, tk):
    rows = lhs.shape[0]
    carry = (jnp.full((rows, 1), NEG_INF, F32), jnp.zeros((rows, 1), F32), jnp.zeros((rows, LANES), F32))
    for k_ref, v_ref in segs:
        nk = k_ref.shape[0]
        t = min(tk, nk)
        for i in range(nk // t):
            carry = _online_step(carry, lhs, k_ref[i * t:(i + 1) * t, :], v_ref[i * t:(i + 1) * t, :])
    _, l, acc = carry
    return acc / l


def _attn_a_kernel(lam_init, n_seg, tk, q_ref, dl_ref, sg_ref, *refs):
    segs = [(refs[2 * i], refs[2 * i + 1]) for i in range(n_seg)]
    o_ref = refs[2 * n_seg]
    tq = q_ref.shape[0]
    o = _flash(_split_heads(q_ref[...]), segs, tk)
    dl = dl_ref[...]
    lam = (jnp.exp(jnp.sum(dl[0:1] * dl[1:2], axis=1, keepdims=True))
           - jnp.exp(jnp.sum(dl[2:3] * dl[3:4], axis=1, keepdims=True)) + lam_init)
    y = o[:tq] - lam * o[tq:]
    o_ref[...] = (_rms(y, sg_ref[...]) * (1.0 - lam_init)).astype(o_ref.dtype)


def _gqa_lhs(q_ref):
    return jnp.concatenate([_split_heads(q_ref[:, j * LANES:(j + 1) * LANES]) for j in range(Q_GROUP)], axis=0)


def _gqa_store(o, o_ref):
    tq = o_ref.shape[0]
    lo = _lane_lo((tq, LANES))
    for j in range(Q_GROUP):
        a = o[(2 * j) * tq:(2 * j + 1) * tq]
        b = o[(2 * j + 1) * tq:(2 * j + 2) * tq]
        o_ref[:, j * LANES:(j + 1) * LANES] = jnp.where(lo, a, b).astype(o_ref.dtype)


def _attn_b_kernel(n_seg, tk, q_ref, *refs):
    segs = [(refs[2 * i], refs[2 * i + 1]) for i in range(n_seg)]
    o_ref = refs[2 * n_seg]
    _gqa_store(_flash(_gqa_lhs(q_ref), segs, tk), o_ref)


def _attn_c_kernel(span, has_lat, sink_ref, q_ref, *refs):
    if has_lat:
        kl_ref, vl_ref, kc_ref, vc_ref, o_ref = refs
    else:
        kc_ref, vc_ref, o_ref = refs
    tq = q_ref.shape[0]
    lhs = _gqa_lhs(q_ref)
    rows = lhs.shape[0]
    sink = jnp.concatenate(
        [jnp.full((tq, 1), sink_ref[n * Q_GROUP + j] * LOG2E, F32) for j in range(Q_GROUP) for n in range(KV_HEADS)],
        axis=0)
    kc = kc_ref[...]
    vc = vc_ref[...]
    s_c = _qk(lhs, kc)
    m = jnp.maximum(jnp.max(s_c, axis=1, keepdims=True), sink)
    if has_lat:
        n_lat = kl_ref.shape[0]
        q0 = pl.program_id(1) * tq
        start = pl.multiple_of(jnp.clip(q0 - WINDOW, 0, n_lat - span), WINDOW)
        kw = kl_ref[pl.ds(start, span), :]
        vw = vl_ref[pl.ds(start, span), :]
        s_w = _qk(lhs, kw)
        qi = lax.broadcasted_iota(jnp.int32, (rows, span), 0) & (tq - 1)
        ki = lax.broadcasted_iota(jnp.int32, (rows, span), 1)
        dist = qi - ki + (q0 - start)
        s_w = jnp.where(jnp.abs(dist) <= WINDOW, s_w, NEG_INF)
        m = jnp.maximum(m, jnp.max(s_w, axis=1, keepdims=True))
    p_c = jnp.exp2((s_c - m).astype(BF16))
    l = jnp.sum(p_c.astype(F32), axis=1, keepdims=True) + jnp.exp2(sink - m)
    acc = jnp.dot(p_c, vc, preferred_element_type=F32)
    if has_lat:
        p_w = jnp.exp2((s_w - m).astype(BF16))
        l = l + jnp.sum(p_w.astype(F32), axis=1, keepdims=True)
        acc = acc + jnp.dot(p_w, vw, preferred_element_type=F32)
    _gqa_store(acc / l, o_ref)


def _mixers(p, seq, ctx_len, n_batch, lam_init, diff_lambda, subln, sink_perm, ctx_queries):
    n_lat = n_batch * seq
    cb = PROJ_TN // LANES
    if ctx_queries:
        tq, nq = ctx_len, 1
        tqa, nqa = tq, nq
        q_row = lambda b, i: n_lat // ctx_len + b
        qa_row = q_row
    else:
        tq = _pow2_tile(128, seq)
        nq = seq // tq
        q_row = lambda b, i: b * nq + i
        tqa = _pow2_tile(512, seq)
        nqa = seq // tqa
        qa_row = lambda b, i: b * nqa + i
    out_rows = n_batch * nq * tq
    tk = 512

    def kv_specs(col, grid_rank):
        def wrap(f):
            return (lambda b, h, i: f(b, h)) if grid_rank == 3 else (lambda b, i: f(b, 0))
        specs = []
        if not ctx_queries:
            specs.append(pl.BlockSpec((seq, LANES), wrap(lambda b, h: (b, col(h)))))
        specs.append(pl.BlockSpec((ctx_len, LANES), wrap(lambda b, h: (n_lat // ctx_len + b, col(h)))))
        return specs

    n_seg = 1 if ctx_queries else 2

    ka = kv_specs(lambda h: T_AK * cb + h, 3)
    va = kv_specs(lambda h: T_AV * cb + h, 3)
    ya = pl.pallas_call(
        functools.partial(_attn_a_kernel, lam_init, n_seg, tk),
        grid=(n_batch, A_HEADS, nqa),
        in_specs=[pl.BlockSpec((tqa, LANES), lambda b, h, i: (qa_row(b, i), T_AQ * cb + h)),
                  pl.BlockSpec(diff_lambda.shape, lambda b, h, i: (0, 0)),
                  pl.BlockSpec((1, LANES), lambda b, h, i: (0, 0))]
        + [s for kv in zip(ka, va) for s in kv],
        out_specs=pl.BlockSpec((tqa, LANES), lambda b, h, i: (b * nqa + i, h)),
        out_shape=jax.ShapeDtypeStruct((out_rows, A_HEADS * LANES), BF16),
        compiler_params=_cparams(("parallel", "parallel", "arbitrary")),
        name="attn_a_ctx" if ctx_queries else "attn_a",
    )(p, diff_lambda, subln.reshape(1, LANES), *([p] * (2 * n_seg)))

    kb = kv_specs(lambda h: T_KV * cb + 0, 2)
    vb = kv_specs(lambda h: T_KV * cb + 2, 2)
    yb = pl.pallas_call(
        functools.partial(_attn_b_kernel, n_seg, tk),
        grid=(n_batch, nq),
        in_specs=[pl.BlockSpec((tq, PROJ_TN), lambda b, i: (q_row(b, i), T_BQ))]
        + [s for kv in zip(kb, vb) for s in kv],
        out_specs=pl.BlockSpec((tq, PROJ_TN), lambda b, i: (b * nq + i, 0)),
        out_shape=jax.ShapeDtypeStruct((out_rows, PROJ_TN), BF16),
        compiler_params=_cparams(("parallel", "arbitrary")),
        name="attn_b_ctx" if ctx_queries else "attn_b",
    )(p, *([p] * (2 * n_seg)))

    kc = kv_specs(lambda h: T_KV * cb + 1, 2)
    vc = kv_specs(lambda h: T_KV * cb + 3, 2)
    span = min(tq + 2 * WINDOW, seq)
    yc = pl.pallas_call(
        functools.partial(_attn_c_kernel, span, not ctx_queries),
        grid=(n_batch, nq),
        in_specs=[pl.BlockSpec(memory_space=pltpu.SMEM),
                  pl.BlockSpec((tq, PROJ_TN), lambda b, i: (q_row(b, i), T_CQ))]
        + [s for kv in zip(kc, vc) for s in kv],
        out_specs=pl.BlockSpec((tq, PROJ_TN), lambda b, i: (b * nq + i, 0)),
        out_shape=jax.ShapeDtypeStruct((out_rows, PROJ_TN), BF16),
        compiler_params=_cparams(("parallel", "arbitrary")),
        name="attn_c_ctx" if ctx_queries else "attn_c",
    )(sink_perm, p, *([p] * (2 * n_seg)))
    return ya, yb, yc


def _merge_kernel(ya_ref, yb_ref, yc_ref, g0_ref, g1_ref, g2_ref, wb_ref, wo_ref, x_ref, gm_ref, ng1_ref,
                  ng2_ref, scf_ref, shf_ref, rw_ref, rb_ref, xo_ref, ho_ref, ti_ref, tw_ref):
    mix = None
    for y_ref, g_ref, i in ((ya_ref, g0_ref, 0), (yb_ref, g1_ref, 1), (yc_ref, g2_ref, 2)):
        t = _sigmoid(g_ref[...].astype(F32)) * jnp.dot(y_ref[...], wb_ref[i], preferred_element_type=F32)
        mix = t if mix is None else mix + t
    y = jnp.dot(mix.astype(BF16), wo_ref[...], preferred_element_type=F32)
    xn = x_ref[...] + gm_ref[0] * _rms(y, ng1_ref[...])
    xo_ref[...] = xn
    h = _rms(xn, ng2_ref[...]) * (1.0 + scf_ref[0]) + shf_ref[0]
    ho_ref[...] = h.astype(ho_ref.dtype)
    logits = jnp.dot(h, rw_ref[...], preferred_element_type=F32,
                     precision=lax.Precision.HIGHEST) + rb_ref[...]
    lane = lax.broadcasted_iota(jnp.int32, logits.shape, 1)
    lane_f = lane.astype(F32)
    lg = jnp.where(lane < N_EXPERTS, logits, -jnp.inf)
    sel_i = jnp.zeros(logits.shape, F32)
    sel_w = jnp.zeros(logits.shape, F32)
    denom = None
    top = None
    for k in range(TOP_K):
        m = jnp.max(lg, axis=1, keepdims=True)
        idx = jnp.min(jnp.where(lg == m, lane_f, float(LANES)), axis=1, keepdims=True)
        top = m if top is None else top
        e = jnp.exp(m - top)
        denom = e if denom is None else denom + e
        sel_i = jnp.where(lane == k, idx, sel_i)
        sel_w = jnp.where(lane == k, e, sel_w)
        lg = jnp.where(lane_f == idx, -jnp.inf, lg)
    ti_ref[...] = sel_i.astype(jnp.int32)
    tw_ref[...] = sel_w / denom


def _merge(ya, yb, yc, p, wb, wo, xc, gm, ng1, ng2, scf, shf, rw, rb, seq, n_batch):
    n = ya.shape[0]
    d = xc.shape[1]
    tm = _pow2_tile(512, seq, n - n_batch * seq)
    seg = _seg_map(seq // tm, n_batch)
    row = lambda i: (i, 0)
    vec = lambda i: (0, 0)
    gate0 = T_GATE * PROJ_TN // d
    ne = rw.shape[1]
    return pl.pallas_call(
        _merge_kernel,
        grid=(n // tm,),
        in_specs=[pl.BlockSpec((tm, PROJ_TN), row)] * 3
        + [pl.BlockSpec((tm, d), lambda i, k=k: (i, gate0 + k)) for k in range(3)]
        + [pl.BlockSpec(wb.shape, lambda i: (0, 0, 0)),
           pl.BlockSpec(wo.shape, vec),
           pl.BlockSpec((tm, d), row),
           pl.BlockSpec((1, 1, d), seg),
           pl.BlockSpec((1, d), vec),
           pl.BlockSpec((1, d), vec),
           pl.BlockSpec((1, 1, d), seg),
           pl.BlockSpec((1, 1, d), seg),
           pl.BlockSpec(rw.shape, vec),
           pl.BlockSpec((1, ne), vec)],
        out_specs=[pl.BlockSpec((tm, d), row), pl.BlockSpec((tm, d), row), pl.BlockSpec((tm, ne), row),
                   pl.BlockSpec((tm, ne), row)],
        out_shape=[jax.ShapeDtypeStruct((n, d), F32), jax.ShapeDtypeStruct((n, d), F32),
                   jax.ShapeDtypeStruct((n, ne), jnp.int32), jax.ShapeDtypeStruct((n, ne), F32)],
        compiler_params=_cparams(("parallel",)),
        name="merge",
    )(ya, yb, yc, p, p, p, wb, wo, xc, gm, ng1.reshape(1, d), ng2.reshape(1, d), scf, shf, rw, rb)


def _row_copies(n_rows, make):
    def start():
        for r in range(n_rows):
            make(r).start()

    def wait():
        for r in range(n_rows):
            make(r).wait()
    return start, wait


def _moe_kernel(be_ref, nu_ref, tok_ref, tok_next_ref, drow_ref, dcol_ref, h_hbm, w1_ref, b1_ref, w2_ref, b2_ref,
                u_hbm, xbuf, ybuf, w1b, w2b, gsem, ssem):
    i = pl.program_id(0)
    n_used = nu_ref[0]
    tm, d = xbuf.shape[1:]
    slot = i % 2

    def gather(idx_ref, s):
        return _row_copies(
            tm, lambda r: pltpu.make_async_copy(h_hbm.at[idx_ref[0, 0, r]], xbuf.at[s, r], gsem.at[s]))

    def scatter(s):
        def make(r):
            col = pl.multiple_of(dcol_ref[0, 0, r], d)
            return pltpu.make_async_copy(ybuf.at[s, r], u_hbm.at[drow_ref[0, 0, r], pl.ds(col, d)], ssem.at[s])
        return _row_copies(tm, make)

    @pl.when(i < n_used)
    def _():
        @pl.when(i == 0)
        def _():
            gather(tok_ref, slot)[0]()
            ybuf[1] = jnp.zeros(ybuf.shape[1:], ybuf.dtype)
            tail = 2 * tm // TOP_K
            for k in range(TOP_K):
                fill = pltpu.make_async_copy(ybuf.at[1, pl.ds(0, tail)],
                                             u_hbm.at[pl.ds(u_hbm.shape[0] - tail, tail), pl.ds(k * d, d)], ssem.at[1])
                fill.start()
                fill.wait()

        @pl.when(i + 1 < n_used)
        def _():
            gather(tok_next_ref, 1 - slot)[0]()

        gather(tok_ref, slot)[1]()

        @pl.when(i >= 2)
        def _():
            scatter(slot)[1]()

        @pl.when((i == 0) | (be_ref[i] != be_ref[jnp.maximum(i - 1, 0)]))
        def _():
            w1b[...] = w1_ref[0].astype(BF16)
            w2b[...] = w2_ref[0].astype(BF16)

        f = w2_ref.shape[1]
        x = xbuf[slot].astype(BF16)
        gu = jnp.dot(x, w1b[...], preferred_element_type=F32) + b1_ref[0]
        gate = jnp.minimum(gu[:, :f], SWIGLU_LIMIT)
        up = jnp.clip(gu[:, f:], -SWIGLU_LIMIT, SWIGLU_LIMIT)
        act = (up + 1.0) * (gate * _sigmoid(SWIGLU_ALPHA * gate))
        y = jnp.dot(act.astype(BF16), w2b[...], preferred_element_type=F32) + b2_ref[0]
        ybuf[slot] = y
        scatter(slot)[0]()

        @pl.when(i == n_used - 1)
        def _():
            @pl.when(i >= 1)
            def _():
                scatter(1 - slot)[1]()
            scatter(slot)[1]()


def _moe_blocks(h, row_tok, row_dst, block_e, n_used, layer, w1, b1, w2, b2, n_out):
    d = h.shape[1]
    depth, ne, _, f2 = w1.shape
    f = w2.shape[2]
    nb = row_tok.shape[0]
    blk = lambda i, be, nu: (jnp.minimum(i, nu[0] - 1), 0, 0)
    blk_next = lambda i, be, nu: (jnp.minimum(i + 1, nu[0] - 1), 0, 0)
    wexp = lambda i, be, nu: (layer, be[i], 0, 0)
    smem_blk = lambda m: pl.BlockSpec((1, 1, MOE_TM), m, memory_space=pltpu.SMEM)
    return pl.pallas_call(
        _moe_kernel,
        grid_spec=pltpu.PrefetchScalarGridSpec(
            num_scalar_prefetch=2,
            grid=(nb,),
            in_specs=[smem_blk(blk), smem_blk(blk_next), smem_blk(blk), smem_blk(blk),
                      pl.BlockSpec(memory_space=pl.ANY),
                      pl.BlockSpec((None, 1, d, f2), wexp),
                      pl.BlockSpec((None, 1, 1, f2), wexp),
                      pl.BlockSpec((None, 1, f, d), wexp),
                      pl.BlockSpec((None, 1, 1, d), wexp)],
            out_specs=pl.BlockSpec(memory_space=pl.ANY),
            scratch_shapes=[pltpu.VMEM((2, MOE_TM, d), F32), pltpu.VMEM((2, MOE_TM, d), F32),
                            pltpu.VMEM((d, f2), BF16), pltpu.VMEM((f, d), BF16),
                            pltpu.SemaphoreType.DMA((2,)), pltpu.SemaphoreType.DMA((2,))],
        ),
        out_shape=jax.ShapeDtypeStruct((n_out // TOP_K, TOP_K * d), F32),
        compiler_params=_cparams(("arbitrary",)),
        name="moe",
    )(block_e, n_used, row_tok, row_tok, row_dst >> TOP_K_SHIFT, (row_dst & (TOP_K - 1)) * d, h,
      w1, b1.reshape(depth, ne, 1, f2), w2, b2.reshape(depth, ne, 1, d))


def _moe(h, top_idx, layer, w1, b1, w2, b2):
    n, d = h.shape
    n_slots = n * TOP_K
    flat_e = top_idx.reshape(-1).astype(jnp.int32)
    order = jnp.argsort(flat_e).astype(jnp.int32)
    counts = jnp.sum(flat_e[:, None] == jnp.arange(N_EXPERTS, dtype=jnp.int32), axis=0, dtype=jnp.int32)
    padded = (counts + MOE_TM - 1) // MOE_TM * MOE_TM
    start = jnp.cumsum(counts) - counts
    pad_end = jnp.cumsum(padded)
    pad_start = pad_end - padded
    nb = -(-n_slots // MOE_TM) + N_EXPERTS
    rows = nb * MOE_TM
    block_row0 = jnp.arange(nb, dtype=jnp.int32) * MOE_TM
    block_e = jnp.minimum(jnp.sum(pad_end[None, :] <= block_row0[:, None], axis=1, dtype=jnp.int32), N_EXPERTS - 1)
    n_used = (pad_end[-1:] // MOE_TM).astype(jnp.int32)
    row_e = jnp.repeat(block_e, MOE_TM)
    row = jnp.arange(rows, dtype=jnp.int32)
    row_j = row - pad_start[row_e]
    row_valid = row_j < counts[row_e]
    row_slot = order[jnp.clip(start[row_e] + row_j, 0, n_slots - 1)]
    row_tok = jnp.where(row_valid, row_slot // TOP_K, 0)
    row_dst = jnp.where(row_valid, row_slot, n_slots + row % (2 * MOE_TM))
    shape3 = (nb, 1, MOE_TM)
    return _moe_blocks(h, row_tok.reshape(shape3), row_dst.reshape(shape3), block_e, n_used, layer, w1, b1, w2, b2,
                       n_slots + 2 * MOE_TM)


def _resid_kernel(with_next, x_ref, u_ref, tw_ref, gf_ref, ng3_ref, *refs):
    d = x_ref.shape[1]
    tw = tw_ref[...]
    y = u_ref[:, 0:d] * tw[:, 0:1]
    for k in range(1, TOP_K):
        y = y + u_ref[:, k * d:(k + 1) * d] * tw[:, k:k + 1]
    xn = x_ref[...] + gf_ref[0] * _rms(y, ng3_ref[...])
    if with_next:
        ng0_ref, sc_ref, sh_ref, xo_ref, ho_ref = refs
        ho_ref[...] = (_rms(xn, ng0_ref[...]) * (1.0 + sc_ref[0]) + sh_ref[0]).astype(ho_ref.dtype)
    else:
        (xo_ref,) = refs
    xo_ref[...] = xn


def _resid(xc, u, tw, gf, ng3, nxt, seq, n_batch):
    n, d = xc.shape
    tm = _pow2_tile(256, seq, n - n_batch * seq)
    seg = _seg_map(seq // tm, n_batch)
    row = lambda i: (i, 0)
    vec = lambda i: (0, 0)
    in_specs = [pl.BlockSpec((tm, d), row), pl.BlockSpec((tm, TOP_K * d), row), pl.BlockSpec((tm, tw.shape[1]), row),
                pl.BlockSpec((1, 1, d), seg), pl.BlockSpec((1, d), vec)]
    args = [xc, u, tw, gf, ng3.reshape(1, d)]
    out_specs = [pl.BlockSpec((tm, d), row)]
    out_shape = [jax.ShapeDtypeStruct((n, d), F32)]
    if nxt is not None:
        ng0, sc, sh = nxt
        in_specs += [pl.BlockSpec((1, d), vec), pl.BlockSpec((1, 1, d), seg), pl.BlockSpec((1, 1, d), seg)]
        args += [ng0.reshape(1, d), sc, sh]
        out_specs.append(pl.BlockSpec((tm, d), row))
        out_shape.append(jax.ShapeDtypeStruct((n, d), BF16))
    return pl.pallas_call(
        functools.partial(_resid_kernel, nxt is not None),
        grid=(n // tm,),
        in_specs=in_specs, out_specs=out_specs, out_shape=out_shape,
        compiler_params=_cparams(("parallel",)),
        name="resid",
    )(*args)


def _gqa_perm():
    idx = np.arange(KV_HEADS * Q_GROUP * HEAD_DIM).reshape(KV_HEADS, Q_GROUP, HEAD_DIM)
    return idx.transpose(1, 0, 2).reshape(-1)


def _w_in_perm():
    a_qk = A_HEADS * 2 * HEAD_DIM
    sizes = dict(aq=a_qk, ak=a_qk, av=A_HEADS * 2 * HEAD_DIM, bq=512, bk=128, bv=128, cq=512, ck=128, cv=128)
    off, o = {}, 0
    for k in ("aq", "ak", "av", "bq", "bk", "bv", "cq", "ck", "cv"):
        off[k] = o
        o += sizes[k]
    seg = lambda k: off[k] + np.arange(sizes[k])
    g = _gqa_perm()
    return np.concatenate([seg("aq"), seg("ak"), off["bq"] + g, off["cq"] + g,
                           seg("bk"), seg("ck"), seg("bv"), seg("cv"), seg("av")]), o


def _rope_tables(seq, n_ctx_rows):
    rows = seq // GRID_W
    row = jnp.repeat(jnp.arange(rows, dtype=F32), GRID_W)
    col = jnp.tile(jnp.arange(GRID_W, dtype=F32), rows)
    inv_freq = ROPE_THETA ** (-jnp.arange(AXIS_PAIRS, dtype=F32) / AXIS_PAIRS)
    ang_r = row[:, None] * inv_freq
    ang_c = col[:, None] * inv_freq
    ang = jnp.concatenate([ang_r, ang_r, ang_c, ang_c], axis=-1)
    cos, sin = jnp.cos(ang), jnp.sin(ang)
    first_half = (np.arange(HEAD_DIM) % (2 * AXIS_PAIRS)) < AXIS_PAIRS
    sa = jnp.where(first_half, -sin, 0.0)
    sb = jnp.where(first_half, 0.0, sin)
    return [jnp.tile(t, (1, LANES // HEAD_DIM)) for t in (cos, sa, sb)]


def kernel(x, c, ctx, c_ctx, ada_w, ada_b, norm_g, w_in, diff_lambda, diff_subln, qk_norm, sink, w_branch, w_out,
           router_w, router_b, w1, b1, w2, b2):
    n_batch, seq, d = x.shape
    ctx_len = ctx.shape[1]
    depth = ada_w.shape[0]
    n_lat, n_ctx = n_batch * seq, n_batch * ctx_len

    mod_rows = -(-(n_batch + 1) // 8) * 8
    cc = jnp.zeros((mod_rows, d), F32).at[:n_batch].set(c).at[n_batch].set(c_ctx)
    mod = _adaln(cc, ada_w, ada_b).reshape(depth, mod_rows, 6, 1, d).transpose(0, 2, 1, 3, 4)

    perm, n_qkv = _w_in_perm()
    perm = np.concatenate([perm, n_qkv + np.arange(w_in.shape[2] - n_qkv)])
    gperm = _gqa_perm()
    cos_l, sa_l, sb_l = _rope_tables(seq, n_ctx)
    ones = jnp.ones((n_ctx, LANES), F32)
    zeros = jnp.zeros((n_ctx, LANES), F32)
    cos_t = jnp.concatenate([jnp.tile(cos_l, (n_batch, 1)), ones])
    sa_t = jnp.concatenate([jnp.tile(sa_l, (n_batch, 1)), zeros])
    sb_t = jnp.concatenate([jnp.tile(sb_l, (n_batch, 1)), zeros])
    ne_pad = -(-N_EXPERTS // LANES) * LANES

    xc = jnp.concatenate([x.reshape(n_lat, d), ctx.reshape(n_ctx, d)])
    sh_m, sc_m = mod[0, 0], mod[0, 1]
    h = _norm_mod(xc, norm_g[0, 0], sc_m, sh_m, seq, n_batch)

    for l in range(depth):
        last = l == depth - 1
        lam_init = 0.8 - 0.6 * math.exp(-0.3 * l)
        sh_m, sc_m, g_m, sh_f, sc_f, g_f = (mod[l, k] for k in range(6))
        w_l = w_in[l][:, perm].astype(BF16)
        gq = jnp.tile(qk_norm[l, 0], LANES // HEAD_DIM).reshape(1, LANES)
        gk = jnp.tile(qk_norm[l, 1], LANES // HEAD_DIM).reshape(1, LANES)
        p = _proj(h, w_l, cos_t, sa_t, sb_t, gq, gk)

        ya, yb, yc = _mixers(p, seq, ctx_len, n_batch, lam_init, diff_lambda[l], diff_subln[l], sink[l], False)
        if not last:
            ya_c, yb_c, yc_c = _mixers(p, seq, ctx_len, n_batch, lam_init, diff_lambda[l], diff_subln[l], sink[l],
                                       True)
            ya, yb, yc = (jnp.concatenate(t) for t in ((ya, ya_c), (yb, yb_c), (yc, yc_c)))
        n_rows = ya.shape[0]

        wb = jnp.stack([w_branch[l, 0], w_branch[l, 1][gperm], w_branch[l, 2][gperm]]).astype(BF16)
        rw = jnp.zeros((d, ne_pad), F32).at[:, :N_EXPERTS].set(router_w[l])
        rb = jnp.zeros((1, ne_pad), F32).at[0, :N_EXPERTS].set(router_b[l])
        xc, h_f, top_i, top_w = _merge(ya, yb, yc, p, wb, w_out[l].astype(BF16), xc, g_m, norm_g[l, 1],
                                       norm_g[l, 2], sc_f, sh_f, rw, rb, seq, n_batch)

        u = _moe(h_f, top_i[:, :TOP_K], l, w1, b1, w2, b2)
        if last:
            (xc,) = _resid(xc, u, top_w, g_f, norm_g[l, 3], None, seq, n_batch)
        else:
            xc, h = _resid(xc, u, top_w, g_f, norm_g[l, 3], (norm_g[l + 1, 0], mod[l + 1, 1], mod[l + 1, 0]),
                           seq, n_batch)
    return xc[:n_lat].reshape(n_batch, seq, d)
```

```python
import functools
import math

import jax
import jax.numpy as jnp
import numpy as np
from jax import lax
from jax.experimental import pallas as pl
from jax.experimental.pallas import tpu as pltpu

F32 = jnp.float32
BF16 = jnp.bfloat16

HEAD_DIM = 64
LANES = 128
GRID_W = 64
AXIS_PAIRS = HEAD_DIM // 4
ROPE_THETA = 10000.0
EPS = 1e-6
NEG_INF = -1e30
WINDOW = 128
A_HEADS = 4
KV_HEADS = 2
Q_GROUP = 4
N_EXPERTS = 32
TOP_K = 4
TOP_K_SHIFT = 2
SWIGLU_LIMIT = 7.0
SWIGLU_ALPHA = 1.702
LOG2E = math.log2(math.e)
Q_SCALE = HEAD_DIM ** -0.5 * LOG2E
PROJ_TN = 512
MOE_TM = 256
VMEM_LIMIT = 56 * 1024 * 1024

T_AQ, T_AK, T_BQ, T_CQ, T_KV, T_AV, T_GATE = 0, 1, 2, 3, 4, 5, 6


def _cparams(sem):
    return pltpu.CompilerParams(dimension_semantics=sem, vmem_limit_bytes=VMEM_LIMIT)


def _rms(x, g):
    return x * lax.rsqrt(jnp.mean(x * x, axis=-1, keepdims=True) + EPS) * g


def _sigmoid(x):
    return 1.0 / (1.0 + jnp.exp(-x))


def _pow2_tile(cap, *sizes):
    t = cap
    while any(s % t for s in sizes):
        t //= 2
    return t


def _adaln_kernel(c_ref, w_ref, b_ref, o_ref):
    c = c_ref[...]
    a = c * _sigmoid(c)
    o_ref[0] = jnp.dot(a, w_ref[0], preferred_element_type=F32,
                       precision=lax.Precision.HIGHEST) + b_ref[0]


def _adaln(cc, ada_w, ada_b):
    depth, d, n = ada_w.shape
    rows = cc.shape[0]
    tn = _pow2_tile(1024, n // 6)
    return pl.pallas_call(
        _adaln_kernel,
        grid=(depth, n // tn),
        in_specs=[pl.BlockSpec((rows, d), lambda l, j: (0, 0)),
                  pl.BlockSpec((1, d, tn), lambda l, j: (l, 0, j)),
                  pl.BlockSpec((1, 1, tn), lambda l, j: (l, 0, j))],
        out_specs=pl.BlockSpec((1, rows, tn), lambda l, j: (l, 0, j)),
        out_shape=jax.ShapeDtypeStruct((depth, rows, n), F32),
        compiler_params=_cparams(("parallel", "parallel")),
        name="adaln",
    )(cc, ada_w, ada_b.reshape(depth, 1, n))


def _norm_mod_kernel(x_ref, g_ref, sc_ref, sh_ref, h_ref):
    h = _rms(x_ref[...], g_ref[...]) * (1.0 + sc_ref[0]) + sh_ref[0]
    h_ref[...] = h.astype(h_ref.dtype)


def _seg_map(tiles_per_batch, n_batch):
    return lambda i: (jnp.minimum(i // tiles_per_batch, n_batch), 0, 0)


def _norm_mod(xc, g, sc, sh, seq, n_batch):
    n, d = xc.shape
    tm = _pow2_tile(1024, seq, n - n_batch * seq)
    seg = _seg_map(seq // tm, n_batch)
    return pl.pallas_call(
        _norm_mod_kernel,
        grid=(n // tm,),
        in_specs=[pl.BlockSpec((tm, d), lambda i: (i, 0)),
                  pl.BlockSpec((1, d), lambda i: (0, 0)),
                  pl.BlockSpec((1, 1, d), seg),
                  pl.BlockSpec((1, 1, d), seg)],
        out_specs=pl.BlockSpec((tm, d), lambda i: (i, 0)),
        out_shape=jax.ShapeDtypeStruct((n, d), BF16),
        compiler_params=_cparams(("parallel",)),
        name="norm_mod",
    )(xc, g.reshape(1, d), sc, sh)


_ROPE_S, _ROPE, _NORMQ, _NORMK, _PLAIN = range(5)
_TILE_OPS = {T_AQ: [_ROPE_S] * 4, T_AK: [_ROPE] * 4, T_BQ: [_NORMQ] * 4, T_CQ: [_ROPE_S] * 4,
             T_KV: [_NORMK, _ROPE, _PLAIN, _PLAIN]}


def _proj_kernel(h_ref, w_ref, cos_ref, sa_ref, sb_ref, gq_ref, gk_ref, o_ref, acc_ref):
    j = pl.program_id(1)
    acc_ref[...] = jnp.dot(h_ref[...], w_ref[...], preferred_element_type=F32)

    def rope(x):
        return (x * cos_ref[...] + pltpu.roll(x, LANES - 16, 1) * sa_ref[...]
                + pltpu.roll(x, 16, 1) * sb_ref[...])

    def head_norm(x, g):
        lo = lax.broadcasted_iota(jnp.int32, x.shape, 1) < HEAD_DIM
        x2 = x * x
        s_lo = jnp.sum(jnp.where(lo, x2, 0.0), axis=1, keepdims=True)
        s_hi = jnp.sum(jnp.where(lo, 0.0, x2), axis=1, keepdims=True)
        ms = jnp.where(lo, s_lo, s_hi) * (1.0 / HEAD_DIM)
        return x * lax.rsqrt(ms + EPS) * g

    def chunk(op, x):
        if op == _ROPE_S:
            return rope(x) * Q_SCALE
        if op == _ROPE:
            return rope(x)
        if op == _NORMQ:
            return rope(head_norm(x, gq_ref[...])) * Q_SCALE
        if op == _NORMK:
            return rope(head_norm(x, gk_ref[...]))
        return x

    for t, ops in _TILE_OPS.items():
        @pl.when(j == t)
        def _(ops=ops):
            for c, op in enumerate(ops):
                cs = slice(c * LANES, (c + 1) * LANES)
                o_ref[:, cs] = chunk(op, acc_ref[:, cs]).astype(o_ref.dtype)

    @pl.when(j >= T_AV)
    def _():
        o_ref[...] = acc_ref[...].astype(o_ref.dtype)


def _proj(h, w, cos, sa, sb, gq, gk):
    n, d = h.shape
    cols = w.shape[1]
    tm = _pow2_tile(1024, n)
    row = lambda i, j: (i, 0)
    vec = lambda i, j: (0, 0)
    return pl.pallas_call(
        _proj_kernel,
        grid=(n // tm, cols // PROJ_TN),
        in_specs=[pl.BlockSpec((tm, d), row),
                  pl.BlockSpec((d, PROJ_TN), lambda i, j: (0, j)),
                  pl.BlockSpec((tm, LANES), row),
                  pl.BlockSpec((tm, LANES), row),
                  pl.BlockSpec((tm, LANES), row),
                  pl.BlockSpec((1, LANES), vec),
                  pl.BlockSpec((1, LANES), vec)],
        out_specs=pl.BlockSpec((tm, PROJ_TN), lambda i, j: (i, j)),
        out_shape=jax.ShapeDtypeStruct((n, cols), BF16),
        scratch_shapes=[pltpu.VMEM((tm, PROJ_TN), F32)],
        compiler_params=_cparams(("parallel", "arbitrary")),
        name="proj",
    )(h, w, cos, sa, sb, gq, gk)


def _lane_lo(shape):
    return lax.broadcasted_iota(jnp.int32, shape, 1) < HEAD_DIM


def _split_heads(qv):
    lo = _lane_lo(qv.shape)
    zero = jnp.zeros_like(qv)
    return jnp.concatenate([jnp.where(lo, qv, zero), jnp.where(lo, zero, qv)], axis=0)


def _qk(lhs, k):
    return lax.dot_general(lhs, k, (((1,), (1,)), ((), ())), preferred_element_type=F32)


def _online_step(carry, lhs, k, v):
    m, l, acc = carry
    s = _qk(lhs, k)
    m_new = jnp.maximum(m, jnp.max(s, axis=1, keepdims=True))
    alpha = jnp.exp2(m - m_new)
    p = jnp.exp2((s - m_new).astype(BF16))
    l = alpha * l + jnp.sum(p.astype(F32), axis=1, keepdims=True)
    acc = alpha * acc + jnp.dot(p, v, preferred_element_type=F32)
    return m_new, l, acc


def _flash(lhs, segs, tk):
    rows = lhs.shape[0]
    carry = (jnp.full((rows, 1), NEG_INF, F32), jnp.zeros((rows, 1), F32), jnp.zeros((rows, LANES), F32))
    for k_ref, v_ref in segs:
        nk = k_ref.shape[0]
        t = min(tk, nk)
        for i in range(nk // t):
            carry = _online_step(carry, lhs, k_ref[i * t:(i + 1) * t, :], v_ref[i * t:(i + 1) * t, :])
    _, l, acc = carry
    return acc / l


def _attn_a_kernel(lam_init, n_seg, tk, q_ref, dl_ref, sg_ref, *refs):
    segs = [(refs[2 * i], refs[2 * i + 1]) for i in range(n_seg)]
    o_ref = refs[2 * n_seg]
    tq = q_ref.shape[0]
    o = _flash(_split_heads(q_ref[...]), segs, tk)
    dl = dl_ref[...]
    lam = (jnp.exp(jnp.sum(dl[0:1] * dl[1:2], axis=1, keepdims=True))
           - jnp.exp(jnp.sum(dl[2:3] * dl[3:4], axis=1, keepdims=True)) + lam_init)
    y = o[:tq] - lam * o[tq:]
    o_ref[...] = (_rms(y, sg_ref[...]) * (1.0 - lam_init)).astype(o_ref.dtype)


def _gqa_lhs(q_ref):
    return jnp.concatenate([_split_heads(q_ref[:, j * LANES:(j + 1) * LANES]) for j in range(Q_GROUP)], axis=0)


def _gqa_store(o, o_ref):
    tq = o_ref.shape[0]
    lo = _lane_lo((tq, LANES))
    for j in range(Q_GROUP):
        a = o[(2 * j) * tq:(2 * j + 1) * tq]
        b = o[(2 * j + 1) * tq:(2 * j + 2) * tq]
        o_ref[:, j * LANES:(j + 1) * LANES] = jnp.where(lo, a, b).astype(o_ref.dtype)


def _attn_b_kernel(n_seg, tk, q_ref, *refs):
    segs = [(refs[2 * i], refs[2 * i + 1]) for i in range(n_seg)]
    o_ref = refs[2 * n_seg]
    _gqa_store(_flash(_gqa_lhs(q_ref), segs, tk), o_ref)


def _attn_c_kernel(span, has_lat, sink_ref, q_ref, *refs):
    if has_lat:
        kl_ref, vl_ref, kc_ref, vc_ref, o_ref = refs
    else:
        kc_ref, vc_ref, o_ref = refs
    tq = q_ref.shape[0]
    lhs = _gqa_lhs(q_ref)
    rows = lhs.shape[0]
    sink = jnp.concatenate(
        [jnp.full((tq, 1), sink_ref[n * Q_GROUP + j] * LOG2E, F32) for j in range(Q_GROUP) for n in range(KV_HEADS)],
        axis=0)
    kc = kc_ref[...]
    vc = vc_ref[...]
    s_c = _qk(lhs, kc)
    m = jnp.maximum(jnp.max(s_c, axis=1, keepdims=True), sink)
    if has_lat:
        n_lat = kl_ref.shape[0]
        q0 = pl.program_id(1) * tq
        start = pl.multiple_of(jnp.clip(q0 - WINDOW, 0, n_lat - span), WINDOW)
        kw = kl_ref[pl.ds(start, span), :]
        vw = vl_ref[pl.ds(start, span), :]
        s_w = _qk(lhs, kw)
        qi = lax.broadcasted_iota(jnp.int32, (rows, span), 0) & (tq - 1)
        ki = lax.broadcasted_iota(jnp.int32, (rows, span), 1)
        dist = qi - ki + (q0 - start)
        s_w = jnp.where(jnp.abs(dist) <= WINDOW, s_w, NEG_INF)
        m = jnp.maximum(m, jnp.max(s_w, axis=1, keepdims=True))
    p_c = jnp.exp2((s_c - m).astype(BF16))
    l = jnp.sum(p_c.astype(F32), axis=1, keepdims=True) + jnp.exp2(sink - m)
    acc = jnp.dot(p_c, vc, preferred_element_type=F32)
    if has_lat:
        p_w = jnp.exp2((s_w - m).astype(BF16))
        l = l + jnp.sum(p_w.astype(F32), axis=1, keepdims=True)
        acc = acc + jnp.dot(p_w, vw, preferred_element_type=F32)
    _gqa_store(acc / l, o_ref)


def _mixers(p, seq, ctx_len, n_batch, lam_init, diff_lambda, subln, sink_perm, ctx_queries):
    n_lat = n_batch * seq
    cb = PROJ_TN // LANES
    if ctx_queries:
        tq, nq = ctx_len, 1
        tqa, nqa = tq, nq
        q_row = lambda b, i: n_lat // ctx_len + b
        qa_row = q_row
    else:
        tq = _pow2_tile(128, seq)
        nq = seq // tq
        q_row = lambda b, i: b * nq + i
        tqa = _pow2_tile(512, seq)
        nqa = seq // tqa
        qa_row = lambda b, i: b * nqa + i
    out_rows = n_batch * nq * tq
    tk = 512

    def kv_specs(col, grid_rank):
        def wrap(f):
            return (lambda b, h, i: f(b, h)) if grid_rank == 3 else (lambda b, i: f(b, 0))
        specs = []
        if not ctx_queries:
            specs.append(pl.BlockSpec((seq, LANES), wrap(lambda b, h: (b, col(h)))))
        specs.append(pl.BlockSpec((ctx_len, LANES), wrap(lambda b, h: (n_lat // ctx_len + b, col(h)))))
        return specs

    n_seg = 1 if ctx_queries else 2

    ka = kv_specs(lambda h: T_AK * cb + h, 3)
    va = kv_specs(lambda h: T_AV * cb + h, 3)
    ya = pl.pallas_call(
        functools.partial(_attn_a_kernel, lam_init, n_seg, tk),
        grid=(n_batch, A_HEADS, nqa),
        in_specs=[pl.BlockSpec((tqa, LANES), lambda b, h, i: (qa_row(b, i), T_AQ * cb + h)),
                  pl.BlockSpec(diff_lambda.shape, lambda b, h, i: (0, 0)),
                  pl.BlockSpec((1, LANES), lambda b, h, i: (0, 0))]
        + [s for kv in zip(ka, va) for s in kv],
        out_specs=pl.BlockSpec((tqa, LANES), lambda b, h, i: (b * nqa + i, h)),
        out_shape=jax.ShapeDtypeStruct((out_rows, A_HEADS * LANES), BF16),
        compiler_params=_cparams(("parallel", "parallel", "arbitrary")),
        name="attn_a_ctx" if ctx_queries else "attn_a",
    )(p, diff_lambda, subln.reshape(1, LANES), *([p] * (2 * n_seg)))

    kb = kv_specs(lambda h: T_KV * cb + 0, 2)
    vb = kv_specs(lambda h: T_KV * cb + 2, 2)
    yb = pl.pallas_call(
        functools.partial(_attn_b_kernel, n_seg, tk),
        grid=(n_batch, nq),
        in_specs=[pl.BlockSpec((tq, PROJ_TN), lambda b, i: (q_row(b, i), T_BQ))]
        + [s for kv in zip(kb, vb) for s in kv],
        out_specs=pl.BlockSpec((tq, PROJ_TN), lambda b, i: (b * nq + i, 0)),
        out_shape=jax.ShapeDtypeStruct((out_rows, PROJ_TN), BF16),
        compiler_params=_cparams(("parallel", "arbitrary")),
        name="attn_b_ctx" if ctx_queries else "attn_b",
    )(p, *([p] * (2 * n_seg)))

    kc = kv_specs(lambda h: T_KV * cb + 1, 2)
    vc = kv_specs(lambda h: T_KV * cb + 3, 2)
    span = min(tq + 2 * WINDOW, seq)
    yc = pl.pallas_call(
        functools.partial(_attn_c_kernel, span, not ctx_queries),
        grid=(n_batch, nq),
        in_specs=[pl.BlockSpec(memory_space=pltpu.SMEM),
                  pl.BlockSpec((tq, PROJ_TN), lambda b, i: (q_row(b, i), T_CQ))]
        + [s for kv in zip(kc, vc) for s in kv],
        out_specs=pl.BlockSpec((tq, PROJ_TN), lambda b, i: (b * nq + i, 0)),
        out_shape=jax.ShapeDtypeStruct((out_rows, PROJ_TN), BF16),
        compiler_params=_cparams(("parallel", "arbitrary")),
        name="attn_c_ctx" if ctx_queries else "attn_c",
    )(sink_perm, p, *([p] * (2 * n_seg)))
    return ya, yb, yc


def _merge_kernel(ya_ref, yb_ref, yc_ref, g0_ref, g1_ref, g2_ref, wb_ref, wo_ref, x_ref, gm_ref, ng1_ref,
                  ng2_ref, scf_ref, shf_ref, rw_ref, rb_ref, xo_ref, ho_ref, ti_ref, tw_ref):
    mix = None
    for y_ref, g_ref, i in ((ya_ref, g0_ref, 0), (yb_ref, g1_ref, 1), (yc_ref, g2_ref, 2)):
        t = _sigmoid(g_ref[...].astype(F32)) * jnp.dot(y_ref[...], wb_ref[i], preferred_element_type=F32)
        mix = t if mix is None else mix + t
    y = jnp.dot(mix.astype(BF16), wo_ref[...], preferred_element_type=F32)
    xn = x_ref[...] + gm_ref[0] * _rms(y, ng1_ref[...])
    xo_ref[...] = xn
    h = _rms(xn, ng2_ref[...]) * (1.0 + scf_ref[0]) + shf_ref[0]
    ho_ref[...] = h.astype(ho_ref.dtype)
    logits = jnp.dot(h, rw_ref[...], preferred_element_type=F32,
                     precision=lax.Precision.HIGHEST) + rb_ref[...]
    lane = lax.broadcasted_iota(jnp.int32, logits.shape, 1)
    lane_f = lane.astype(F32)
    lg = jnp.where(lane < N_EXPERTS, logits, -jnp.inf)
    sel_i = jnp.zeros(logits.shape, F32)
    sel_w = jnp.zeros(logits.shape, F32)
    denom = None
    top = None
    for k in range(TOP_K):
        m = jnp.max(lg, axis=1, keepdims=True)
        idx = jnp.min(jnp.where(lg == m, lane_f, float(LANES)), axis=1, keepdims=True)
        top = m if top is None else top
        e = jnp.exp(m - top)
        denom = e if denom is None else denom + e
        sel_i = jnp.where(lane == k, idx, sel_i)
        sel_w = jnp.where(lane == k, e, sel_w)
        lg = jnp.where(lane_f == idx, -jnp.inf, lg)
    ti_ref[...] = sel_i.astype(jnp.int32)
    tw_ref[...] = sel_w / denom


def _merge(ya, yb, yc, p, wb, wo, xc, gm, ng1, ng2, scf, shf, rw, rb, seq, n_batch):
    n = ya.shape[0]
    d = xc.shape[1]
    tm = _pow2_tile(512, seq, n - n_batch * seq)
    seg = _seg_map(seq // tm, n_batch)
    row = lambda i: (i, 0)
    vec = lambda i: (0, 0)
    gate0 = T_GATE * PROJ_TN // d
    ne = rw.shape[1]
    return pl.pallas_call(
        _merge_kernel,
        grid=(n // tm,),
        in_specs=[pl.BlockSpec((tm, PROJ_TN), row)] * 3
        + [pl.BlockSpec((tm, d), lambda i, k=k: (i, gate0 + k)) for k in range(3)]
        + [pl.BlockSpec(wb.shape, lambda i: (0, 0, 0)),
           pl.BlockSpec(wo.shape, vec),
           pl.BlockSpec((tm, d), row),
           pl.BlockSpec((1, 1, d), seg),
           pl.BlockSpec((1, d), vec),
           pl.BlockSpec((1, d), vec),
           pl.BlockSpec((1, 1, d), seg),
           pl.BlockSpec((1, 1, d), seg),
           pl.BlockSpec(rw.shape, vec),
           pl.BlockSpec((1, ne), vec)],
        out_specs=[pl.BlockSpec((tm, d), row), pl.BlockSpec((tm, d), row), pl.BlockSpec((tm, ne), row),
                   pl.BlockSpec((tm, ne), row)],
        out_shape=[jax.ShapeDtypeStruct((n, d), F32), jax.ShapeDtypeStruct((n, d), F32),
                   jax.ShapeDtypeStruct((n, ne), jnp.int32), jax.ShapeDtypeStruct((n, ne), F32)],
        compiler_params=_cparams(("parallel",)),
        name="merge",
    )(ya, yb, yc, p, p, p, wb, wo, xc, gm, ng1.reshape(1, d), ng2.reshape(1, d), scf, shf, rw, rb)


def _row_copies(n_rows, make, n_queues=1):
    def start():
        for r in range(n_rows):
            make(r).start(priority=r % n_queues)

    def wait():
        for r in range(n_rows):
            make(r).wait()
    return start, wait


def _moe_kernel(be_ref, nu_ref, tok_ref, tok_next_ref, drow_ref, dcol_ref, h_hbm, w1_ref, b1_ref, w2_ref, b2_ref,
                u_hbm, xbuf, ybuf, w1b, w2b, gsem, ssem):
    i = pl.program_id(0)
    n_used = nu_ref[0]
    tm, d = xbuf.shape[1:]
    slot = i % 2

    def gather(idx_ref, s):
        return _row_copies(
            tm, lambda r: pltpu.make_async_copy(h_hbm.at[idx_ref[0, 0, r]], xbuf.at[s, r], gsem.at[s]))

    def scatter(s):
        def make(r):
            col = pl.multiple_of(dcol_ref[0, 0, r], d)
            return pltpu.make_async_copy(ybuf.at[s, r], u_hbm.at[drow_ref[0, 0, r], pl.ds(col, d)], ssem.at[s])
        return _row_copies(tm, make, n_queues=2)

    @pl.when(i < n_used)
    def _():
        @pl.when(i == 0)
        def _():
            gather(tok_ref, slot)[0]()
            ybuf[1] = jnp.zeros(ybuf.shape[1:], ybuf.dtype)
            tail = 2 * tm // TOP_K
            for k in range(TOP_K):
                fill = pltpu.make_async_copy(ybuf.at[1, pl.ds(0, tail)],
                                             u_hbm.at[pl.ds(u_hbm.shape[0] - tail, tail), pl.ds(k * d, d)], ssem.at[1])
                fill.start()
                fill.wait()

        @pl.when(i + 1 < n_used)
        def _():
            gather(tok_next_ref, 1 - slot)[0]()

        gather(tok_ref, slot)[1]()

        @pl.when(i >= 2)
        def _():
            scatter(slot)[1]()

        @pl.when((i == 0) | (be_ref[i] != be_ref[jnp.maximum(i - 1, 0)]))
        def _():
            w1b[...] = w1_ref[0].astype(BF16)
            w2b[...] = w2_ref[0].astype(BF16)

        f = w2_ref.shape[1]
        x = xbuf[slot].astype(BF16)
        gu = jnp.dot(x, w1b[...], preferred_element_type=F32) + b1_ref[0]
        gate = jnp.minimum(gu[:, :f], SWIGLU_LIMIT)
        up = jnp.clip(gu[:, f:], -SWIGLU_LIMIT, SWIGLU_LIMIT)
        act = (up + 1.0) * (gate * _sigmoid(SWIGLU_ALPHA * gate))
        y = jnp.dot(act.astype(BF16), w2b[...], preferred_element_type=F32) + b2_ref[0]
        ybuf[slot] = y
        scatter(slot)[0]()

        @pl.when(i == n_used - 1)
        def _():
            @pl.when(i >= 1)
            def _():
                scatter(1 - slot)[1]()
            scatter(slot)[1]()


def _moe_blocks(h, row_tok, row_dst, block_e, n_used, layer, w1, b1, w2, b2, n_out):
    d = h.shape[1]
    depth, ne, _, f2 = w1.shape
    f = w2.shape[2]
    nb = row_tok.shape[0]
    blk = lambda i, be, nu: (jnp.minimum(i, nu[0] - 1), 0, 0)
    blk_next = lambda i, be, nu: (jnp.minimum(i + 1, nu[0] - 1), 0, 0)
    wexp = lambda i, be, nu: (layer, be[i], 0, 0)
    smem_blk = lambda m: pl.BlockSpec((1, 1, MOE_TM), m, memory_space=pltpu.SMEM)
    return pl.pallas_call(
        _moe_kernel,
        grid_spec=pltpu.PrefetchScalarGridSpec(
            num_scalar_prefetch=2,
            grid=(nb,),
            in_specs=[smem_blk(blk), smem_blk(blk_next), smem_blk(blk), smem_blk(blk),
                      pl.BlockSpec(memory_space=pl.ANY),
                      pl.BlockSpec((None, 1, d, f2), wexp),
                      pl.BlockSpec((None, 1, 1, f2), wexp),
                      pl.BlockSpec((None, 1, f, d), wexp),
                      pl.BlockSpec((None, 1, 1, d), wexp)],
            out_specs=pl.BlockSpec(memory_space=pl.ANY),
            scratch_shapes=[pltpu.VMEM((2, MOE_TM, d), F32), pltpu.VMEM((2, MOE_TM, d), F32),
                            pltpu.VMEM((d, f2), BF16), pltpu.VMEM((f, d), BF16),
                            pltpu.SemaphoreType.DMA((2,)), pltpu.SemaphoreType.DMA((2,))],
        ),
        out_shape=jax.ShapeDtypeStruct((n_out // TOP_K, TOP_K * d), F32),
        compiler_params=_cparams(("arbitrary",)),
        name="moe",
    )(block_e, n_used, row_tok, row_tok, row_dst >> TOP_K_SHIFT, (row_dst & (TOP_K - 1)) * d, h,
      w1, b1.reshape(depth, ne, 1, f2), w2, b2.reshape(depth, ne, 1, d))


def _moe(h, top_idx, layer, w1, b1, w2, b2):
    n, d = h.shape
    n_slots = n * TOP_K
    flat_e = top_idx.reshape(-1).astype(jnp.int32)
    order = jnp.argsort(flat_e).astype(jnp.int32)
    counts = jnp.sum(flat_e[:, None] == jnp.arange(N_EXPERTS, dtype=jnp.int32), axis=0, dtype=jnp.int32)
    padded = (counts + MOE_TM - 1) // MOE_TM * MOE_TM
    start = jnp.cumsum(counts) - counts
    pad_end = jnp.cumsum(padded)
    pad_start = pad_end - padded
    nb = -(-n_slots // MOE_TM) + N_EXPERTS
    rows = nb * MOE_TM
    block_row0 = jnp.arange(nb, dtype=jnp.int32) * MOE_TM
    block_e = jnp.minimum(jnp.sum(pad_end[None, :] <= block_row0[:, None], axis=1, dtype=jnp.int32), N_EXPERTS - 1)
    n_used = (pad_end[-1:] // MOE_TM).astype(jnp.int32)
    row_e = jnp.repeat(block_e, MOE_TM)
    row = jnp.arange(rows, dtype=jnp.int32)
    row_j = row - pad_start[row_e]
    row_valid = row_j < counts[row_e]
    row_slot = order[jnp.clip(start[row_e] + row_j, 0, n_slots - 1)]
    row_tok = jnp.where(row_valid, row_slot // TOP_K, 0)
    row_dst = jnp.where(row_valid, row_slot, n_slots + row % (2 * MOE_TM))
    shape3 = (nb, 1, MOE_TM)
    return _moe_blocks(h, row_tok.reshape(shape3), row_dst.reshape(shape3), block_e, n_used, layer, w1, b1, w2, b2,
                       n_slots + 2 * MOE_TM)


def _resid_kernel(with_next, x_ref, u_ref, tw_ref, gf_ref, ng3_ref, *refs):
    d = x_ref.shape[1]
    tw = tw_ref[...]
    y = u_ref[:, 0:d] * tw[:, 0:1]
    for k in range(1, TOP_K):
        y = y + u_ref[:, k * d:(k + 1) * d] * tw[:, k:k + 1]
    xn = x_ref[...] + gf_ref[0] * _rms(y, ng3_ref[...])
    if with_next:
        ng0_ref, sc_ref, sh_ref, xo_ref, ho_ref = refs
        ho_ref[...] = (_rms(xn, ng0_ref[...]) * (1.0 + sc_ref[0]) + sh_ref[0]).astype(ho_ref.dtype)
    else:
        (xo_ref,) = refs
    xo_ref[...] = xn


def _resid(xc, u, tw, gf, ng3, nxt, seq, n_batch):
    n, d = xc.shape
    tm = _pow2_tile(256, seq, n - n_batch * seq)
    seg = _seg_map(seq // tm, n_batch)
    row = lambda i: (i, 0)
    vec = lambda i: (0, 0)
    in_specs = [pl.BlockSpec((tm, d), row), pl.BlockSpec((tm, TOP_K * d), row), pl.BlockSpec((tm, tw.shape[1]), row),
                pl.BlockSpec((1, 1, d), seg), pl.BlockSpec((1, d), vec)]
    args = [xc, u, tw, gf, ng3.reshape(1, d)]
    out_specs = [pl.BlockSpec((tm, d), row)]
    out_shape = [jax.ShapeDtypeStruct((n, d), F32)]
    if nxt is not None:
        ng0, sc, sh = nxt
        in_specs += [pl.BlockSpec((1, d), vec), pl.BlockSpec((1, 1, d), seg), pl.BlockSpec((1, 1, d), seg)]
        args += [ng0.reshape(1, d), sc, sh]
        out_specs.append(pl.BlockSpec((tm, d), row))
        out_shape.append(jax.ShapeDtypeStruct((n, d), BF16))
    return pl.pallas_call(
        functools.partial(_resid_kernel, nxt is not None),
        grid=(n // tm,),
        in_specs=in_specs, out_specs=out_specs, out_shape=out_shape,
        compiler_params=_cparams(("parallel",)),
        name="resid",
    )(*args)


def _gqa_perm():
    idx = np.arange(KV_HEADS * Q_GROUP * HEAD_DIM).reshape(KV_HEADS, Q_GROUP, HEAD_DIM)
    return idx.transpose(1, 0, 2).reshape(-1)


def _w_in_perm():
    a_qk = A_HEADS * 2 * HEAD_DIM
    sizes = dict(aq=a_qk, ak=a_qk, av=A_HEADS * 2 * HEAD_DIM, bq=512, bk=128, bv=128, cq=512, ck=128, cv=128)
    off, o = {}, 0
    for k in ("aq", "ak", "av", "bq", "bk", "bv", "cq", "ck", "cv"):
        off[k] = o
        o += sizes[k]
    seg = lambda k: off[k] + np.arange(sizes[k])
    g = _gqa_perm()
    return np.concatenate([seg("aq"), seg("ak"), off["bq"] + g, off["cq"] + g,
                           seg("bk"), seg("ck"), seg("bv"), seg("cv"), seg("av")]), o


def _rope_tables(seq, n_ctx_rows):
    rows = seq // GRID_W
    row = jnp.repeat(jnp.arange(rows, dtype=F32), GRID_W)
    col = jnp.tile(jnp.arange(GRID_W, dtype=F32), rows)
    inv_freq = ROPE_THETA ** (-jnp.arange(AXIS_PAIRS, dtype=F32) / AXIS_PAIRS)
    ang_r = row[:, None] * inv_freq
    ang_c = col[:, None] * inv_freq
    ang = jnp.concatenate([ang_r, ang_r, ang_c, ang_c], axis=-1)
    cos, sin = jnp.cos(ang), jnp.sin(ang)
    first_half = (np.arange(HEAD_DIM) % (2 * AXIS_PAIRS)) < AXIS_PAIRS
    sa = jnp.where(first_half, -sin, 0.0)
    sb = jnp.where(first_half, 0.0, sin)
    return [jnp.tile(t, (1, LANES // HEAD_DIM)) for t in (cos, sa, sb)]


def kernel(x, c, ctx, c_ctx, ada_w, ada_b, norm_g, w_in, diff_lambda, diff_subln, qk_norm, sink, w_branch, w_out,
           router_w, router_b, w1, b1, w2, b2):
    n_batch, seq, d = x.shape
    ctx_len = ctx.shape[1]
    depth = ada_w.shape[0]
    n_lat, n_ctx = n_batch * seq, n_batch * ctx_len

    mod_rows = -(-(n_batch + 1) // 8) * 8
    cc = jnp.zeros((mod_rows, d), F32).at[:n_batch].set(c).at[n_batch].set(c_ctx)
    mod = _adaln(cc, ada_w, ada_b).reshape(depth, mod_rows, 6, 1, d).transpose(0, 2, 1, 3, 4)

    perm, n_qkv = _w_in_perm()
    perm = np.concatenate([perm, n_qkv + np.arange(w_in.shape[2] - n_qkv)])
    gperm = _gqa_perm()
    cos_l, sa_l, sb_l = _rope_tables(seq, n_ctx)
    ones = jnp.ones((n_ctx, LANES), F32)
    zeros = jnp.zeros((n_ctx, LANES), F32)
    cos_t = jnp.concatenate([jnp.tile(cos_l, (n_batch, 1)), ones])
    sa_t = jnp.concatenate([jnp.tile(sa_l, (n_batch, 1)), zeros])
    sb_t = jnp.concatenate([jnp.tile(sb_l, (n_batch, 1)), zeros])
    ne_pad = -(-N_EXPERTS // LANES) * LANES

    xc = jnp.concatenate([x.reshape(n_lat, d), ctx.reshape(n_ctx, d)])
    sh_m, sc_m = mod[0, 0], mod[0, 1]
    h = _norm_mod(xc, norm_g[0, 0], sc_m, sh_m, seq, n_batch)

    for l in range(depth):
        last = l == depth - 1
        lam_init = 0.8 - 0.6 * math.exp(-0.3 * l)
        sh_m, sc_m, g_m, sh_f, sc_f, g_f = (mod[l, k] for k in range(6))
        w_l = w_in[l][:, perm].astype(BF16)
        gq = jnp.tile(qk_norm[l, 0], LANES // HEAD_DIM).reshape(1, LANES)
        gk = jnp.tile(qk_norm[l, 1], LANES // HEAD_DIM).reshape(1, LANES)
        p = _proj(h, w_l, cos_t, sa_t, sb_t, gq, gk)

        ya, yb, yc = _mixers(p, seq, ctx_len, n_batch, lam_init, diff_lambda[l], diff_subln[l], sink[l], False)
        if not last:
            ya_c, yb_c, yc_c = _mixers(p, seq, ctx_len, n_batch, lam_init, diff_lambda[l], diff_subln[l], sink[l],
                                       True)
            ya, yb, yc = (jnp.concatenate(t) for t in ((ya, ya_c), (yb, yb_c), (yc, yc_c)))
        n_rows = ya.shape[0]

        wb = jnp.stack([w_branch[l, 0], w_branch[l, 1][gperm], w_branch[l, 2][gperm]]).astype(BF16)
        rw = jnp.zeros((d, ne_pad), F32).at[:, :N_EXPERTS].set(router_w[l])
        rb = jnp.zeros((1, ne_pad), F32).at[0, :N_EXPERTS].set(router_b[l])
        xc, h_f, top_i, top_w = _merge(ya, yb, yc, p, wb, w_out[l].astype(BF16), xc, g_m, norm_g[l, 1],
                                       norm_g[l, 2], sc_f, sh_f, rw, rb, seq, n_batch)

        u = _moe(h_f, top_i[:, :TOP_K], l, w1, b1, w2, b2)
        if last:
            (xc,) = _resid(xc, u, top_w, g_f, norm_g[l, 3], None, seq, n_batch)
        else:
            xc, h = _resid(xc, u, top_w, g_f, norm_g[l, 3], (norm_g[l + 1, 0], mod[l + 1, 1], mod[l + 1, 0]),
                           seq, n_batch)
    return xc[:n_lat].reshape(n_batch, seq, d)
```
